```python
import math
import jax, jax.numpy as jnp
from jax import lax
import numpy as np

D_MODEL = 1024
BATCH = 4
SEQ = 8192
DEPTH = 1

MEM_LEN = 256
MIX_WIDTH = D_MODEL
GDN_HEADS = 4
GDN_HEAD_DIM = 128
GDN_WIDTH = GDN_HEADS * GDN_HEAD_DIM
CFM_WIDTH = MIX_WIDTH - GDN_WIDTH
GDN_SHORT_CONV = 4
CHUNK = 64
CFM_KERNEL = 31
XATTN_HEADS = 4
XATTN_HEAD_DIM = D_MODEL // XATTN_HEADS
D_FF = 2816
FFN_CONV = 3
EPS = 1e-6
IN_COLS = 4 * GDN_WIDTH + 2 * GDN_HEADS + 2 * CFM_WIDTH

kernel_name = "hybrid_gdn_conformer_xattn_convffn"


def rmsnorm(x, g):
    xf = x.astype(jnp.float32)
    y = xf * lax.rsqrt(jnp.mean(xf * xf, axis=-1, keepdims=True) + EPS)
    return (y * g.astype(jnp.float32)).astype(x.dtype)


def layernorm(x, g, b):
    xf = x.astype(jnp.float32)
    mu = jnp.mean(xf, axis=-1, keepdims=True)
    xc = xf - mu
    var = jnp.mean(xc * xc, axis=-1, keepdims=True)
    return (xc * lax.rsqrt(var + EPS) * g.astype(jnp.float32) + b.astype(jnp.float32)).astype(x.dtype)


def l2norm(x):
    return x * lax.rsqrt(jnp.sum(x * x, axis=-1, keepdims=True) + EPS)


def causal_dwconv(x, w):
    K, C = w.shape
    return lax.conv_general_dilated(
        x, w[:, None, :].astype(x.dtype), window_strides=(1,), padding=[(K - 1, 0)],
        dimension_numbers=("NWC", "WIO", "NWC"), feature_group_count=C)


def gated_delta_rule_chunked(q, k, v, g, beta):
    B, T, H, Dk = q.shape
    Dv = v.shape[-1]
    N = T // CHUNK

    def to_chunks(a):
        a = a.reshape((B, N, CHUNK, H) + a.shape[3:])
        return jnp.moveaxis(a, (1, 3), (0, 2))

    qc = to_chunks(q * (Dk ** -0.5))
    kc = to_chunks(k)
    vc = to_chunks(v)
    bc = to_chunks(beta)
    gc = jnp.cumsum(to_chunks(g), axis=-1)

    idx = jnp.arange(CHUNK)
    causal = idx[:, None] >= idx[None, :]
    strict = idx[:, None] > idx[None, :]
    diff = gc[..., :, None] - gc[..., None, :]
    decay = jnp.where(causal, jnp.exp(jnp.where(causal, diff, 0.0)), 0.0)

    kb = kc * bc[..., None]
    L = jnp.where(strict, jnp.einsum('nbhid,nbhjd->nbhij', kb, kc) * decay, 0.0)
    eye = jnp.eye(CHUNK, dtype=jnp.float32)
    rhs = jnp.concatenate([vc * bc[..., None], kb * jnp.exp(gc)[..., None]], axis=-1)
    sol = lax.linalg.triangular_solve(eye + L, rhs, left_side=True, lower=True, unit_diagonal=True)
    u = sol[..., :Dv]
    w = sol[..., Dv:]
    attn_intra = jnp.einsum('nbhid,nbhjd->nbhij', qc, kc) * decay

    def step(S, inp):
        q_i, k_i, u_i, w_i, g_i, a_i = inp
        v_new = u_i - jnp.einsum('bhcd,bhde->bhce', w_i, S)
        o = (jnp.einsum('bhcd,bhde->bhce', q_i * jnp.exp(g_i)[..., None], S)
             + jnp.einsum('bhij,bhje->bhie', a_i, v_new))
        g_last = g_i[..., -1]
        k_dec = k_i * jnp.exp(g_last[..., None] - g_i)[..., None]
        S = S * jnp.exp(g_last)[..., None, None] + jnp.einsum('bhcd,bhce->bhde', k_dec, v_new)
        return S, o

    S0 = jnp.zeros((B, H, Dk, Dv), jnp.float32)
    _, o = lax.scan(step, S0, (qc, kc, u, w, gc, attn_intra))
    return jnp.moveaxis(o, (0, 2), (1, 3)).reshape(B, T, H, Dv)


def hybrid_mixer(h, w_in, gdn_conv_w, a_log, dt_bias, gdn_norm_g,
                 cfm_dw_w, cfm_dw_b, cfm_ln_g, cfm_ln_b, w_out):
    B, T, _ = h.shape
    z = h @ w_in
    s1 = 3 * GDN_WIDTH
    s2 = 4 * GDN_WIDTH
    s3 = s2 + GDN_HEADS
    s4 = s3 + GDN_HEADS
    qkv, gate, a_logit, b_logit, cfm_in = jnp.split(z, [s1, s2, s3, s4], axis=-1)

    qkv = jax.nn.silu(causal_dwconv(qkv, gdn_conv_w)).astype(jnp.float32)
    q, k, v = jnp.split(qkv, 3, axis=-1)
    shp = (B, T, GDN_HEADS, GDN_HEAD_DIM)
    q = l2norm(q.reshape(shp))
    k = l2norm(k.reshape(shp))
    v = v.reshape(shp)
    g = -jnp.exp(a_log.astype(jnp.float32)) * jax.nn.softplus(
        a_logit.astype(jnp.float32) + dt_bias.astype(jnp.float32))
    beta = jax.nn.sigmoid(b_logit.astype(jnp.float32))
    o = gated_delta_rule_chunked(q, k, v, g, beta)
    o = rmsnorm(o, gdn_norm_g) * jax.nn.silu(gate.astype(jnp.float32).reshape(shp))
    o_gdn = o.reshape(B, T, GDN_WIDTH).astype(h.dtype)

    c = jax.nn.glu(cfm_in, axis=-1)
    c = causal_dwconv(c, cfm_dw_w) + cfm_dw_b
    c = jax.nn.silu(layernorm(c, cfm_ln_g, cfm_ln_b))

    return jnp.concatenate([o_gdn, c], axis=-1) @ w_out


def memory_cross_attention(h, mem_n, w_q, w_kv, w_o):
    B, T, _ = h.shape
    M = mem_n.shape[1]
    q = (h @ w_q).reshape(B, T, XATTN_HEADS, XATTN_HEAD_DIM)
    k, v = jnp.split(mem_n @ w_kv, 2, axis=-1)
    k = k.reshape(B, M, XATTN_HEADS, XATTN_HEAD_DIM)
    v = v.reshape(B, M, XATTN_HEADS, XATTN_HEAD_DIM)
    s = jnp.einsum('bthd,bmhd->bhtm', q, k).astype(jnp.float32) * (XATTN_HEAD_DIM ** -0.5)
    p = jax.nn.softmax(s, axis=-1).astype(h.dtype)
    o = jnp.einsum('bhtm,bmhd->bthd', p, v).reshape(B, T, D_MODEL)
    return o @ w_o


def conv_glu_ffn(h, w_up, conv_w, conv_b, w_down):
    u = causal_dwconv(h @ w_up, conv_w) + conv_b
    a, b = jnp.split(u, 2, axis=-1)
    return (jax.nn.silu(a) * b) @ w_down


def setup_inputs(seed: int = 0) -> dict:
    key = jax.random.key(seed)
    ks = jax.random.split(key, 24)
    f32 = jnp.float32

    def nrm(k, shape, scale):
        return jax.random.normal(k, shape, f32) * scale

    x = nrm(ks[0], (BATCH, SEQ, D_MODEL), 1.0)
    mem = nrm(ks[1], (BATCH, MEM_LEN, D_MODEL), 1.0)
    norm_g = 1.0 + nrm(ks[2], (DEPTH, 6, D_MODEL), 0.05)
    w_in = nrm(ks[3], (DEPTH, D_MODEL, IN_COLS), D_MODEL ** -0.5)
    gdn_conv_w = nrm(ks[4], (DEPTH, GDN_SHORT_CONV, 3 * GDN_WIDTH), GDN_SHORT_CONV ** -0.5)
    gdn_a_log = jnp.log(jax.random.uniform(ks[5], (DEPTH, GDN_HEADS), f32, 1.0, 16.0))
    dt = jnp.exp(jax.random.uniform(ks[6], (DEPTH, GDN_HEADS), f32,
                                    math.log(1e-3), math.log(1e-1)))
    gdn_dt_bias = dt + jnp.log(-jnp.expm1(-dt))
    gdn_norm_g = 1.0 + nrm(ks[7], (DEPTH, GDN_HEAD_DIM), 0.05)
    cfm_dw_w = nrm(ks[8], (DEPTH, CFM_KERNEL, CFM_WIDTH), CFM_KERNEL ** -0.5)
    cfm_dw_b = nrm(ks[9], (DEPTH, CFM_WIDTH), 0.02)
    cfm_ln_g = 1.0 + nrm(ks[10], (DEPTH, CFM_WIDTH), 0.05)
    cfm_ln_b = nrm(ks[11], (DEPTH, CFM_WIDTH), 0.02)
    w_out = nrm(ks[12], (DEPTH, MIX_WIDTH, D_MODEL), MIX_WIDTH ** -0.5)
    mem_norm_g = 1.0 + nrm(ks[13], (DEPTH, D_MODEL), 0.05)
    xa_w_q = nrm(ks[14], (DEPTH, D_MODEL, D_MODEL), D_MODEL ** -0.5)
    xa_w_kv = nrm(ks[15], (DEPTH, D_MODEL, 2 * D_MODEL), D_MODEL ** -0.5)
    xa_w_o = nrm(ks[16], (DEPTH, D_MODEL, D_MODEL), D_MODEL ** -0.5)
    ffn_w_up = nrm(ks[17], (DEPTH, D_MODEL, 2 * D_FF), D_MODEL ** -0.5)
    ffn_conv_w = nrm(ks[18], (DEPTH, FFN_CONV, 2 * D_FF), FFN_CONV ** -0.5)
    ffn_conv_b = nrm(ks[19], (DEPTH, 2 * D_FF), 0.02)
    ffn_w_down = nrm(ks[20], (DEPTH, D_FF, D_MODEL), D_FF ** -0.5)
    return {"x": x, "mem": mem, "norm_g": norm_g, "w_in": w_in, "gdn_conv_w": gdn_conv_w,
            "gdn_a_log": gdn_a_log, "gdn_dt_bias": gdn_dt_bias, "gdn_norm_g": gdn_norm_g,
            "cfm_dw_w": cfm_dw_w, "cfm_dw_b": cfm_dw_b, "cfm_ln_g": cfm_ln_g, "cfm_ln_b": cfm_ln_b,
            "w_out": w_out, "mem_norm_g": mem_norm_g, "xa_w_q": xa_w_q, "xa_w_kv": xa_w_kv,
            "xa_w_o": xa_w_o, "ffn_w_up": ffn_w_up, "ffn_conv_w": ffn_conv_w,
            "ffn_conv_b": ffn_conv_b, "ffn_w_down": ffn_w_down}


def reference(x, mem, norm_g, w_in, gdn_conv_w, gdn_a_log, gdn_dt_bias, gdn_norm_g,
              cfm_dw_w, cfm_dw_b, cfm_ln_g, cfm_ln_b, w_out, mem_norm_g,
              xa_w_q, xa_w_kv, xa_w_o, ffn_w_up, ffn_conv_w, ffn_conv_b, ffn_w_down):
    h = x
    for l in range(DEPTH):
        g = norm_g[l]
        y = hybrid_mixer(rmsnorm(h, g[0]), w_in[l], gdn_conv_w[l], gdn_a_log[l], gdn_dt_bias[l],
                         gdn_norm_g[l], cfm_dw_w[l], cfm_dw_b[l], cfm_ln_g[l], cfm_ln_b[l], w_out[l])
        h = h + rmsnorm(y, g[1])
        y = memory_cross_attention(rmsnorm(h, g[2]), rmsnorm(mem, mem_norm_g[l]),
                                   xa_w_q[l], xa_w_kv[l], xa_w_o[l])
        h = h + rmsnorm(y, g[3])
        y = conv_glu_ffn(rmsnorm(h, g[4]), ffn_w_up[l], ffn_conv_w[l], ffn_conv_b[l], ffn_w_down[l])
        h = h + rmsnorm(y, g[5])
    return h
```

```python
import functools

import jax
import jax.numpy as jnp
from jax import lax
from jax.experimental import pallas as pl
from jax.experimental.pallas import tpu as pltpu

EPS = 1e-6
GDN_HEADS = 4
GDN_HEAD_DIM = 128
GDN_WIDTH = GDN_HEADS * GDN_HEAD_DIM
GDN_SHORT_CONV = 4
CHUNK = 64
XATTN_HEADS = 4

F32 = jnp.float32
BF16 = jnp.bfloat16

SUBLANES = 8
VMEM_LIMIT_BYTES = 56 * 1024 * 1024

NT_DIMS = (((1,), (1,)), ((), ()))
TN_DIMS = (((0,), (0,)), ((), ()))


def _mm(a, b):
    return jnp.dot(a.astype(BF16), b.astype(BF16), preferred_element_type=F32)


def _mm_nt(a, b):
    return lax.dot_general(a.astype(BF16), b.astype(BF16), NT_DIMS, preferred_element_type=F32)


def _mm_tn(a, b):
    return lax.dot_general(a.astype(BF16), b.astype(BF16), TN_DIMS, preferred_element_type=F32)


def _rms(x, g):
    return x * lax.rsqrt(jnp.mean(x * x, axis=-1, keepdims=True) + EPS) * g


def _sigmoid(x):
    return 1.0 / (1.0 + jnp.exp(-x))


def _silu(x):
    return x * _sigmoid(x)


def _const_spec(shape):
    zeros = (0,) * len(shape)
    return pl.BlockSpec(shape, lambda *_: zeros)


def _params(n_axes):
    return pltpu.CompilerParams(dimension_semantics=("arbitrary",) * n_axes,
                                vmem_limit_bytes=VMEM_LIMIT_BYTES)


def _mem_kv_kernel(mem_ref, g_ref, w_ref, kv_ref):
    mn = _rms(mem_ref[0], g_ref[...])
    kv_ref[0] = _mm(mn, w_ref[...]).astype(BF16)


def _mem_kv(mem, g, w_kv):
    B, M, D = mem.shape
    N = w_kv.shape[1]
    return pl.pallas_call(
        _mem_kv_kernel,
        grid=(B,),
        in_specs=[pl.BlockSpec((1, M, D), lambda b: (b, 0, 0)), _const_spec((1, D)), _const_spec((D, N))],
        out_specs=pl.BlockSpec((1, M, N), lambda b: (b, 0, 0)),
        out_shape=jax.ShapeDtypeStruct((B, M, N), BF16),
        compiler_params=_params(1),
        name="mem_kv",
    )(mem, g, w_kv)


def _mixer_in_kernel(x_ref, g_ref, wqkvg_ref, wab_ref, wcfm_ref, convw_ref, alog_ref, dtb_ref,
                     dww_ref, dwb_ref, lng_ref, lnb_ref,
                     q_ref, k_ref, v_ref, gate_ref, gb_ref, c_ref,
                     zbuf, cbuf, *, tm, cfm_k):
    qkv_w = 3 * GDN_WIDTH
    z_halo = SUBLANES
    c_halo = cbuf.shape[0] - tm

    @pl.when(pl.program_id(1) == 0)
    def _():
        zbuf[0:z_halo, :] = jnp.zeros((z_halo, qkv_w), F32)
        cbuf[0:c_halo, :] = jnp.zeros((c_halo, cbuf.shape[1]), F32)

    hb = _rms(x_ref[0], g_ref[...]).astype(BF16)

    z = jnp.dot(hb, wqkvg_ref[...], preferred_element_type=F32)
    zbuf[z_halo:z_halo + tm, :] = z[:, :qkv_w]
    gate_ref[0] = _silu(z[:, qkv_w:])

    base = z_halo - (GDN_SHORT_CONV - 1)
    acc = convw_ref[0:1, :] * zbuf[base:base + tm, :]
    for j in range(1, GDN_SHORT_CONV):
        acc = acc + convw_ref[j:j + 1, :] * zbuf[base + j:base + j + tm, :]
    zbuf[0:z_halo, :] = zbuf[tm:tm + z_halo, :]
    qkv = _silu(acc)
    q_scale = GDN_HEAD_DIM ** -0.5
    for h in range(GDN_HEADS):
        lo = h * GDN_HEAD_DIM
        qh = qkv[:, lo:lo + GDN_HEAD_DIM]
        kh = qkv[:, GDN_WIDTH + lo:GDN_WIDTH + lo + GDN_HEAD_DIM]
        q_ref[0, :, lo:lo + GDN_HEAD_DIM] = qh * (
            lax.rsqrt(jnp.sum(qh * qh, axis=-1, keepdims=True) + EPS) * q_scale)
        k_ref[0, :, lo:lo + GDN_HEAD_DIM] = kh * lax.rsqrt(jnp.sum(kh * kh, axis=-1, keepdims=True) + EPS)
    v_ref[0] = qkv[:, 2 * GDN_WIDTH:]

    ab = lax.dot_general(wab_ref[...], hb, NT_DIMS, preferred_element_type=F32)
    sp_in = ab + dtb_ref[...]
    softplus = jnp.maximum(sp_in, 0.0) + jnp.log(1.0 + jnp.exp(-jnp.abs(sp_in)))
    g = -jnp.exp(alog_ref[...]) * softplus
    row = lax.broadcasted_iota(jnp.int32, ab.shape, 0)
    gb = jnp.where(row < GDN_HEADS, g, _sigmoid(ab))
    for c in range(tm // CHUNK):
        gb_ref[0, c] = gb[:, c * CHUNK:(c + 1) * CHUNK]

    cw = cbuf.shape[1]
    cz = jnp.dot(hb, wcfm_ref[...], preferred_element_type=F32)
    cbuf[c_halo:c_halo + tm, :] = cz[:, :cw] * _sigmoid(cz[:, cw:])
    base = c_halo - (cfm_k - 1)
    conv = dwb_ref[...] + dww_ref[0:1, :] * cbuf[base:base + tm, :]
    for j in range(1, cfm_k):
        conv = conv + dww_ref[j:j + 1, :] * cbuf[base + j:base + j + tm, :]
    cbuf[0:c_halo, :] = cbuf[tm:tm + c_halo, :]
    mu = jnp.mean(conv, axis=-1, keepdims=True)
    xc = conv - mu
    var = jnp.mean(xc * xc, axis=-1, keepdims=True)
    c_ref[0] = _silu(xc * lax.rsqrt(var + EPS) * lng_ref[...] + lnb_ref[...])


def _mixer_in(x, g0, w_qkvg, w_ab, w_cfm, conv_w, a_log8, dt_bias8, dw_w, dw_b, ln_g, ln_b, *, tm):
    B, T, D = x.shape
    cfm_k, cw = dw_w.shape
    c_halo = -(-(cfm_k - 1) // SUBLANES) * SUBLANES
    row_spec = lambda w: pl.BlockSpec((1, tm, w), lambda b, i: (b, i, 0))
    kernel = functools.partial(_mixer_in_kernel, tm=tm, cfm_k=cfm_k)
    wide = jax.ShapeDtypeStruct((B, T, GDN_WIDTH), F32)
    return pl.pallas_call(
        kernel,
        grid=(B, T // tm),
        in_specs=[row_spec(D), _const_spec(g0.shape), _const_spec(w_qkvg.shape), _const_spec(w_ab.shape),
                  _const_spec(w_cfm.shape), _const_spec(conv_w.shape), _const_spec(a_log8.shape),
                  _const_spec(dt_bias8.shape), _const_spec(dw_w.shape), _const_spec(dw_b.shape),
                  _const_spec(ln_g.shape), _const_spec(ln_b.shape)],
        out_specs=[row_spec(GDN_WIDTH), row_spec(GDN_WIDTH), row_spec(GDN_WIDTH), row_spec(GDN_WIDTH),
                   pl.BlockSpec((1, tm // CHUNK, 2 * GDN_HEADS, CHUNK), lambda b, i: (b, i, 0, 0)),
                   row_spec(cw)],
        out_shape=[wide, wide, wide, wide,
                   jax.ShapeDtypeStruct((B, T // CHUNK, 2 * GDN_HEADS, CHUNK), F32),
                   jax.ShapeDtypeStruct((B, T, cw), F32)],
        scratch_shapes=[pltpu.VMEM((SUBLANES + tm, 3 * GDN_WIDTH), F32),
                        pltpu.VMEM((c_halo + tm, cw), F32)],
        compiler_params=_params(2),
        name="mixer_in",
    )(x, g0, w_qkvg, w_ab, w_cfm, conv_w, a_log8, dt_bias8, dw_w, dw_b, ln_g, ln_b)


def _unit_lower_inverse(L, eye, diag_mask, off_masks):
    Ld = jnp.where(diag_mask, L, 0.0)
    T = eye - Ld
    P = Ld
    for _ in range(3):
        P = _mm(P, P)
        T = T + _mm(T, P)
    for m in off_masks:
        T = T - _mm(_mm(T, jnp.where(m, L, 0.0)), T)
    return T


def _gdn_kernel(q_ref, k_ref, v_ref, gate_ref, gb_ref, ng_ref, o_ref, s_ref, *, n_chunks):
    @pl.when(pl.program_id(1) == 0)
    def _():
        s_ref[...] = jnp.zeros(s_ref.shape, F32)

    C, H, Dh = CHUNK, GDN_HEADS, GDN_HEAD_DIM
    ri = lax.broadcasted_iota(jnp.int32, (C, C), 0)
    ci = lax.broadcasted_iota(jnp.int32, (C, C), 1)
    causal = ri >= ci
    strict = ri > ci
    lower1 = jnp.where(causal, 1.0, 0.0).astype(F32)
    upper1 = jnp.where(ri <= ci, 1.0, 0.0).astype(F32)
    eye = jnp.where(ri == ci, 1.0, 0.0).astype(F32)
    diag_mask = (ri // 16) == (ci // 16)
    off_masks = [((ri // (2 * s)) == (ci // (2 * s))) & ((ri // s) > (ci // s)) for s in (16, 32)]
    hi = lax.Precision.HIGHEST

    for c in range(n_chunks):
        gbc = gb_ref[0, c]
        gc_row = jnp.dot(gbc, upper1, precision=hi, preferred_element_type=F32)
        gc_col = lax.dot_general(lower1, gbc, NT_DIMS, precision=hi, preferred_element_type=F32)
        gb_col = lax.dot_general(eye, gbc, NT_DIMS, precision=hi, preferred_element_type=F32)
        rows = slice(c * C, (c + 1) * C)
        for h in range(H):
            cols = slice(h * Dh, (h + 1) * Dh)
            qh = q_ref[0, rows, cols]
            kh = k_ref[0, rows, cols]
            vh = v_ref[0, rows, cols]
            gcol = gc_col[:, h:h + 1]
            grow = gc_row[h:h + 1, :]
            bcol = gb_col[:, H + h:H + h + 1]
            decay = jnp.where(causal, jnp.exp(jnp.where(causal, gcol - grow, 0.0)), 0.0)
            kb = kh * bcol
            L = jnp.where(strict, _mm_nt(kb, kh) * decay, 0.0)
            T = _unit_lower_inverse(L, eye, diag_mask, off_masks)
            egc = jnp.exp(gcol)
            sol = _mm(T, jnp.concatenate([vh * bcol, kb * egc], axis=1))
            u = sol[:, :Dh]
            w = sol[:, Dh:]
            attn = _mm_nt(qh, kh) * decay
            S = s_ref[h]
            ws_qs = _mm(jnp.concatenate([w, qh * egc], axis=0), S)
            v_new = u - ws_qs[:C]
            o = ws_qs[C:] + _mm(attn, v_new)
            g_last = grow[:, C - 1:C]
            k_dec = kh * jnp.exp(g_last - gcol)
            s_ref[h] = S * jnp.exp(g_last) + _mm_tn(k_dec, v_new)
            o_ref[0, rows, cols] = _rms(o, ng_ref[...]) * gate_ref[0, rows, cols]


def _gdn(q, k, v, gate, gb, norm_g, *, tg):
    B, T, W = q.shape
    n_chunks = tg // CHUNK
    row_spec = pl.BlockSpec((1, tg, W), lambda b, i: (b, i, 0))
    return pl.pallas_call(
        functools.partial(_gdn_kernel, n_chunks=n_chunks),
        grid=(B, T // tg),
        in_specs=[row_spec, row_spec, row_spec, row_spec,
                  pl.BlockSpec((1, n_chunks, 2 * GDN_HEADS, CHUNK), lambda b, i: (b, i, 0, 0)),
                  _const_spec(norm_g.shape)],
        out_specs=row_spec,
        out_shape=jax.ShapeDtypeStruct((B, T, W), F32),
        scratch_shapes=[pltpu.VMEM((GDN_HEADS, GDN_HEAD_DIM, GDN_HEAD_DIM), F32)],
        compiler_params=_params(2),
        name="gdn",
    )(q, k, v, gate, gb, norm_g)


def _attn_kernel(x_ref, og_ref, c_ref, g_ref, wout_ref, kv_ref, wq_ref, wo_ref, h_ref):
    D = x_ref.shape[2]
    dh = D // XATTN_HEADS
    mix = jnp.concatenate([og_ref[0].astype(BF16), c_ref[0].astype(BF16)], axis=1)
    y = jnp.dot(mix, wout_ref[...], preferred_element_type=F32)
    h1 = x_ref[0] + _rms(y, g_ref[1:2, :])

    q = _mm(_rms(h1, g_ref[2:3, :]), wq_ref[...]) * (dh ** -0.5)
    heads = []
    for hd in range(XATTN_HEADS):
        kh = kv_ref[0, :, hd * dh:(hd + 1) * dh]
        vh = kv_ref[0, :, D + hd * dh:D + (hd + 1) * dh]
        s = _mm_nt(q[:, hd * dh:(hd + 1) * dh], kh)
        e = jnp.exp(s - jnp.max(s, axis=-1, keepdims=True))
        heads.append(_mm(e, vh) * (1.0 / jnp.sum(e, axis=-1, keepdims=True)))
    y = _mm(jnp.concatenate(heads, axis=1), wo_ref[...])
    h_ref[0] = h1 + _rms(y, g_ref[3:4, :])


def _attn(x, o_gdn, c, norm_g, w_out, kv, w_q, w_o, *, tm):
    B, T, D = x.shape
    M = kv.shape[1]
    row_spec = lambda w: pl.BlockSpec((1, tm, w), lambda b, i: (b, i, 0))
    return pl.pallas_call(
        _attn_kernel,
        grid=(B, T // tm),
        in_specs=[row_spec(D), row_spec(o_gdn.shape[2]), row_spec(c.shape[2]), _const_spec(norm_g.shape),
                  _const_spec(w_out.shape), pl.BlockSpec((1, M, 2 * D), lambda b, i: (b, 0, 0)),
                  _const_spec(w_q.shape), _const_spec(w_o.shape)],
        out_specs=row_spec(D),
        out_shape=jax.ShapeDtypeStruct((B, T, D), F32),
        compiler_params=_params(2),
        name="attn",
    )(x, o_gdn, c, norm_g, w_out, kv, w_q, w_o)


def _ffn_kernel(h_ref, g_ref, wup_ref, cw_ref, cb_ref, wdn_ref, o_ref, ubuf, carry, *, tm, fc, conv_k):
    F = wdn_ref.shape[0]
    halo = SUBLANES

    @pl.when(pl.program_id(1) == 0)
    def _():
        carry[...] = jnp.zeros(carry.shape, F32)

    h2 = h_ref[0]
    hn = _rms(h2, g_ref[4:5, :]).astype(BF16)

    def conv_cols(lo):
        ubuf[0:halo, :] = carry[:, lo:lo + fc]
        ubuf[halo:halo + tm, :] = jnp.dot(hn, wup_ref[:, lo:lo + fc], preferred_element_type=F32)
        carry[:, lo:lo + fc] = ubuf[tm:tm + halo, :]
        base = halo - (conv_k - 1)
        out = cb_ref[:, lo:lo + fc] + cw_ref[0:1, lo:lo + fc] * ubuf[base:base + tm, :]
        for j in range(1, conv_k):
            out = out + cw_ref[j:j + 1, lo:lo + fc] * ubuf[base + j:base + j + tm, :]
        return out

    acc = jnp.zeros((tm, o_ref.shape[2]), F32)
    for j in range(F // fc):
        a = conv_cols(j * fc)
        b = conv_cols(F + j * fc)
        acc = acc + jnp.dot((_silu(a) * b).astype(BF16), wdn_ref[j * fc:(j + 1) * fc, :],
                            preferred_element_type=F32)
    o_ref[0] = h2 + _rms(acc, g_ref[5:6, :])


def _ffn(h, norm_g, w_up, conv_w, conv_b, w_down, *, tm, fc):
    B, T, D = h.shape
    F = w_down.shape[0]
    row_spec = pl.BlockSpec((1, tm, D), lambda b, i: (b, i, 0))
    kernel = functools.partial(_ffn_kernel, tm=tm, fc=fc, conv_k=conv_w.shape[0])
    return pl.pallas_call(
        kernel,
        grid=(B, T // tm),
        in_specs=[row_spec, _const_spec(norm_g.shape), _const_spec(w_up.shape), _const_spec(conv_w.shape),
                  _const_spec(conv_b.shape), _const_spec(w_down.shape)],
        out_specs=row_spec,
        out_shape=jax.ShapeDtypeStruct((B, T, D), F32),
        scratch_shapes=[pltpu.VMEM((SUBLANES + tm, fc), F32), pltpu.VMEM((SUBLANES, 2 * F), F32)],
        compiler_params=_params(2),
        name="ffn",
    )(h, norm_g, w_up, conv_w, conv_b, w_down)


def _pick_tile(T, pref):
    t = min(T, pref)
    assert T % t == 0 and t % CHUNK == 0, (T, t)
    return t


def _layer(h, mem, g, w_in, gdn_conv_w, a_log, dt_bias, gdn_norm_g, cfm_dw_w, cfm_dw_b, cfm_ln_g, cfm_ln_b,
           w_out, mem_norm_g, xa_w_q, xa_w_kv, xa_w_o, ffn_w_up, ffn_conv_w, ffn_conv_b, ffn_w_down):
    B, T, D = h.shape
    s2 = 4 * GDN_WIDTH
    s4 = s2 + 2 * GDN_HEADS
    w_qkvg = w_in[:, :s2].astype(BF16)
    w_ab = w_in[:, s2:s4].T.astype(BF16)
    w_cfm = w_in[:, s4:].astype(BF16)
    pad = jnp.zeros((GDN_HEADS,), F32)
    a_log8 = jnp.concatenate([a_log.astype(F32), pad])[:, None]
    dt_bias8 = jnp.concatenate([dt_bias.astype(F32), pad])[:, None]

    tm = _pick_tile(T, 256)
    kv = _mem_kv(mem, mem_norm_g[None, :], xa_w_kv.astype(BF16))
    q, k, v, gate, gb, c = _mixer_in(h, g[0:1], w_qkvg, w_ab, w_cfm, gdn_conv_w, a_log8, dt_bias8,
                                     cfm_dw_w, cfm_dw_b[None, :], cfm_ln_g[None, :], cfm_ln_b[None, :], tm=tm)
    o_gdn = _gdn(q, k, v, gate, gb, gdn_norm_g[None, :], tg=_pick_tile(T, 256))
    h = _attn(h, o_gdn, c, g, w_out.astype(BF16), kv, xa_w_q.astype(BF16), xa_w_o.astype(BF16), tm=tm)
    return _ffn(h, g, ffn_w_up.astype(BF16), ffn_conv_w, ffn_conv_b[None, :], ffn_w_down.astype(BF16),
                tm=tm, fc=256)


def kernel(x, mem, norm_g, w_in, gdn_conv_w, gdn_a_log, gdn_dt_bias, gdn_norm_g, cfm_dw_w, cfm_dw_b,
           cfm_ln_g, cfm_ln_b, w_out, mem_norm_g, xa_w_q, xa_w_kv, xa_w_o, ffn_w_up, ffn_conv_w,
           ffn_conv_b, ffn_w_down):
    h = x
    for l in range(norm_g.shape[0]):
        h = _layer(h, mem, norm_g[l], w_in[l], gdn_conv_w[l], gdn_a_log[l], gdn_dt_bias[l], gdn_norm_g[l],
                   cfm_dw_w[l], cfm_dw_b[l], cfm_ln_g[l], cfm_ln_b[l], w_out[l], mem_norm_g[l],
                   xa_w_q[l], xa_w_kv[l], xa_w_o[l], ffn_w_up[l], ffn_conv_w[l], ffn_conv_b[l], ffn_w_down[l])
    return h
```

```python
import functools

import jax
import jax.numpy as jnp
from jax import lax
from jax.experimental import pallas as pl
from jax.experimental.pallas import tpu as pltpu

EPS = 1e-6
GDN_HEADS = 4
GDN_HEAD_DIM = 128
GDN_WIDTH = GDN_HEADS * GDN_HEAD_DIM
GDN_SHORT_CONV = 4
CHUNK = 64
XATTN_HEADS = 4

F32 = jnp.float32
BF16 = jnp.bfloat16

SUBLANES = 8
VMEM_LIMIT_BYTES = 56 * 1024 * 1024

NT_DIMS = (((1,), (1,)), ((), ()))
TN_DIMS = (((0,), (0,)), ((), ()))


def _mm(a, b):
    return jnp.dot(a.astype(BF16), b.astype(BF16), preferred_element_type=F32)


def _mm_nt(a, b):
    return lax.dot_general(a.astype(BF16), b.astype(BF16), NT_DIMS, preferred_element_type=F32)


def _mm_tn(a, b):
    return lax.dot_general(a.astype(BF16), b.astype(BF16), TN_DIMS, preferred_element_type=F32)


def _rms(x, g):
    return x * lax.rsqrt(jnp.mean(x * x, axis=-1, keepdims=True) + EPS) * g


def _sigmoid(x):
    return 1.0 / (1.0 + jnp.exp(-x))


def _silu(x):
    return x * _sigmoid(x)


def _const_spec(shape):
    zeros = (0,) * len(shape)
    return pl.BlockSpec(shape, lambda *_: zeros)


def _params(n_axes):
    return pltpu.CompilerParams(dimension_semantics=("arbitrary",) * n_axes,
                                vmem_limit_bytes=VMEM_LIMIT_BYTES)


def _mem_kv_kernel(mem_ref, g_ref, w_ref, kv_ref):
    mn = _rms(mem_ref[0], g_ref[...])
    kv_ref[0] = _mm(mn, w_ref[...]).astype(BF16)


def _mem_kv(mem, g, w_kv):
    B, M, D = mem.shape
    N = w_kv.shape[1]
    return pl.pallas_call(
        _mem_kv_kernel,
        grid=(B,),
        in_specs=[pl.BlockSpec((1, M, D), lambda b: (b, 0, 0)), _const_spec((1, D)), _const_spec((D, N))],
        out_specs=pl.BlockSpec((1, M, N), lambda b: (b, 0, 0)),
        out_shape=jax.ShapeDtypeStruct((B, M, N), BF16),
        compiler_params=_params(1),
        name="mem_kv",
    )(mem, g, w_kv)


def _mixer_in_kernel(x_ref, g_ref, wqkvg_ref, wab_ref, wcfm_ref, convw_ref, alog_ref, dtb_ref,
                     dww_ref, dwb_ref, lng_ref, lnb_ref,
                     q_ref, k_ref, v_ref, gate_ref, gb_ref, c_ref,
                     zbuf, cbuf, *, tm, cfm_k):
    qkv_w = 3 * GDN_WIDTH
    z_halo = SUBLANES
    c_halo = cbuf.shape[0] - tm

    @pl.when(pl.program_id(1) == 0)
    def _():
        zbuf[0:z_halo, :] = jnp.zeros((z_halo, qkv_w), F32)
        cbuf[0:c_halo, :] = jnp.zeros((c_halo, cbuf.shape[1]), F32)

    hb = _rms(x_ref[0], g_ref[...]).astype(BF16)

    z = jnp.dot(hb, wqkvg_ref[...], preferred_element_type=F32)
    zbuf[z_halo:z_halo + tm, :] = z[:, :qkv_w]
    gate_ref[0] = _silu(z[:, qkv_w:])

    base = z_halo - (GDN_SHORT_CONV - 1)
    acc = convw_ref[0:1, :] * zbuf[base:base + tm, :]
    for j in range(1, GDN_SHORT_CONV):
        acc = acc + convw_ref[j:j + 1, :] * zbuf[base + j:base + j + tm, :]
    zbuf[0:z_halo, :] = zbuf[tm:tm + z_halo, :]
    qkv = _silu(acc)
    q_scale = GDN_HEAD_DIM ** -0.5
    for h in range(GDN_HEADS):
        lo = h * GDN_HEAD_DIM
        qh = qkv[:, lo:lo + GDN_HEAD_DIM]
        kh = qkv[:, GDN_WIDTH + lo:GDN_WIDTH + lo + GDN_HEAD_DIM]
        q_ref[0, :, lo:lo + GDN_HEAD_DIM] = qh * (
            lax.rsqrt(jnp.sum(qh * qh, axis=-1, keepdims=True) + EPS) * q_scale)
        k_ref[0, :, lo:lo + GDN_HEAD_DIM] = kh * lax.rsqrt(jnp.sum(kh * kh, axis=-1, keepdims=True) + EPS)
    v_ref[0] = qkv[:, 2 * GDN_WIDTH:]

    ab = lax.dot_general(wab_ref[...], hb, NT_DIMS, preferred_element_type=F32)
    sp_in = ab + dtb_ref[...]
    softplus = jnp.maximum(sp_in, 0.0) + jnp.log(1.0 + jnp.exp(-jnp.abs(sp_in)))
    g = -jnp.exp(alog_ref[...]) * softplus
    row = lax.broadcasted_iota(jnp.int32, ab.shape, 0)
    gb = jnp.where(row < GDN_HEADS, g, _sigmoid(ab))
    for c in range(tm // CHUNK):
        gb_ref[0, c] = gb[:, c * CHUNK:(c + 1) * CHUNK]

    cw = cbuf.shape[1]
    cz = jnp.dot(hb, wcfm_ref[...], preferred_element_type=F32)
    cbuf[c_halo:c_halo + tm, :] = cz[:, :cw] * _sigmoid(cz[:, cw:])
    base = c_halo - (cfm_k - 1)
    conv = dwb_ref[...] + dww_ref[0:1, :] * cbuf[base:base + tm, :]
    for j in range(1, cfm_k):
        conv = conv + dww_ref[j:j + 1, :] * cbuf[base + j:base + j + tm, :]
    cbuf[0:c_halo, :] = cbuf[tm:tm + c_halo, :]
    mu = jnp.mean(conv, axis=-1, keepdims=True)
    xc = conv - mu
    var = jnp.mean(xc * xc, axis=-1, keepdims=True)
    c_ref[0] = _silu(xc * lax.rsqrt(var + EPS) * lng_ref[...] + lnb_ref[...])


def _mixer_in(x, g0, w_qkvg, w_ab, w_cfm, conv_w, a_log8, dt_bias8, dw_w, dw_b, ln_g, ln_b, *, tm):
    B, T, D = x.shape
    cfm_k, cw = dw_w.shape
    c_halo = -(-(cfm_k - 1) // SUBLANES) * SUBLANES
    row_spec = lambda w: pl.BlockSpec((1, tm, w), lambda b, i: (b, i, 0))
    kernel = functools.partial(_mixer_in_kernel, tm=tm, cfm_k=cfm_k)
    wide = jax.ShapeDtypeStruct((B, T, GDN_WIDTH), F32)
    return pl.pallas_call(
        kernel,
        grid=(B, T // tm),
        in_specs=[row_spec(D), _const_spec(g0.shape), _const_spec(w_qkvg.shape), _const_spec(w_ab.shape),
                  _const_spec(w_cfm.shape), _const_spec(conv_w.shape), _const_spec(a_log8.shape),
                  _const_spec(dt_bias8.shape), _const_spec(dw_w.shape), _const_spec(dw_b.shape),
                  _const_spec(ln_g.shape), _const_spec(ln_b.shape)],
        out_specs=[row_spec(GDN_WIDTH), row_spec(GDN_WIDTH), row_spec(GDN_WIDTH), row_spec(GDN_WIDTH),
                   pl.BlockSpec((1, tm // CHUNK, 2 * GDN_HEADS, CHUNK), lambda b, i: (b, i, 0, 0)),
                   row_spec(cw)],
        out_shape=[wide, wide, wide, wide,
                   jax.ShapeDtypeStruct((B, T // CHUNK, 2 * GDN_HEADS, CHUNK), F32),
                   jax.ShapeDtypeStruct((B, T, cw), F32)],
        scratch_shapes=[pltpu.VMEM((SUBLANES + tm, 3 * GDN_WIDTH), F32),
                        pltpu.VMEM((c_halo + tm, cw), F32)],
        compiler_params=_params(2),
        name="mixer_in",
    )(x, g0, w_qkvg, w_ab, w_cfm, conv_w, a_log8, dt_bias8, dw_w, dw_b, ln_g, ln_b)


def _unit_lower_inverse(Ls, eye, diag_mask, off_masks):
    Ps = [jnp.where(diag_mask, L, 0.0) for L in Ls]
    Ts = [eye - P for P in Ps]
    for _ in range(3):
        Ps = [_mm(P, P) for P in Ps]
        Ts = [T + _mm(T, P) for T, P in zip(Ts, Ps)]
    for m in off_masks:
        Xs = [_mm(T, jnp.where(m, L, 0.0)) for T, L in zip(Ts, Ls)]
        Ts = [T - _mm(X, T) for T, X in zip(Ts, Xs)]
    return Ts


def _gdn_kernel(q_ref, k_ref, v_ref, gate_ref, gb_ref, ng_ref, o_ref, s_ref, *, n_chunks):
    @pl.when(pl.program_id(0) == 0)
    def _():
        s_ref[...] = jnp.zeros(s_ref.shape, F32)

    C, H, Dh = CHUNK, GDN_HEADS, GDN_HEAD_DIM
    nb = q_ref.shape[0]
    units = [(b, h) for b in range(nb) for h in range(H)]
    ri = lax.broadcasted_iota(jnp.int32, (C, C), 0)
    ci = lax.broadcasted_iota(jnp.int32, (C, C), 1)
    causal = ri >= ci
    strict = ri > ci
    lower1 = jnp.where(causal, 1.0, 0.0).astype(F32)
    upper1 = jnp.where(ri <= ci, 1.0, 0.0).astype(F32)
    eye = jnp.where(ri == ci, 1.0, 0.0).astype(F32)
    diag_mask = (ri // 16) == (ci // 16)
    off_masks = [((ri // (2 * s)) == (ci // (2 * s))) & ((ri // s) > (ci // s)) for s in (16, 32)]
    hi = lax.Precision.HIGHEST
    ng = ng_ref[...]

    for c in range(n_chunks):
        rows = slice(c * C, (c + 1) * C)
        gc_row, gc_col, gb_col = [], [], []
        for b in range(nb):
            gbc = gb_ref[b, c]
            gc_row.append(jnp.dot(gbc, upper1, precision=hi, preferred_element_type=F32))
            gc_col.append(lax.dot_general(lower1, gbc, NT_DIMS, precision=hi, preferred_element_type=F32))
            gb_col.append(lax.dot_general(eye, gbc, NT_DIMS, precision=hi, preferred_element_type=F32))

        qgs, ks, kbqs, rhs, decays, kdecs, elast = [], [], [], [], [], [], []
        for b, h in units:
            cols = slice(h * Dh, (h + 1) * Dh)
            qh = q_ref[b, rows, cols]
            kh = k_ref[b, rows, cols]
            vh = v_ref[b, rows, cols]
            gcol = gc_col[b][:, h:h + 1]
            grow = gc_row[b][h:h + 1, :]
            bcol = gb_col[b][:, H + h:H + h + 1]
            g_last = grow[:, C - 1:C]
            egc = jnp.exp(gcol)
            kb = kh * bcol
            decays.append(jnp.where(causal, jnp.exp(jnp.where(causal, gcol - grow, 0.0)), 0.0))
            qgs.append(qh * egc)
            ks.append(kh.astype(BF16))
            kbqs.append(jnp.concatenate([kb, qh], axis=0))
            rhs.append(jnp.concatenate([vh * bcol, kb * egc], axis=1))
            kdecs.append(kh * jnp.exp(g_last - gcol))
            elast.append(jnp.exp(g_last))
        prods = [_mm_nt(a, k) for a, k in zip(kbqs, ks)]
        Ls = [jnp.where(strict, p[:C] * d, 0.0) for p, d in zip(prods, decays)]
        attns = [p[C:] * d for p, d in zip(prods, decays)]
        Ts = _unit_lower_inverse(Ls, eye, diag_mask, off_masks)
        sols = [_mm(T, r) for T, r in zip(Ts, rhs)]

        Ss = [s_ref[i] for i in range(len(units))]
        ws_qs = [_mm(jnp.concatenate([sol[:, Dh:], qg], axis=0), S) for sol, qg, S in zip(sols, qgs, Ss)]
        v_new = [sol[:, :Dh] - x[:C] for sol, x in zip(sols, ws_qs)]
        v_new_b = [x.astype(BF16) for x in v_new]
        outs = [x[C:] + _mm(a, vb) for x, a, vb in zip(ws_qs, attns, v_new_b)]
        for i, (S, e, kd, vb) in enumerate(zip(Ss, elast, kdecs, v_new_b)):
            s_ref[i] = S * e + _mm_tn(kd, vb)
        for (b, h), o in zip(units, outs):
            cols = slice(h * Dh, (h + 1) * Dh)
            o_ref[b, rows, cols] = _rms(o, ng) * gate_ref[b, rows, cols]


def _gdn(q, k, v, gate, gb, norm_g, *, tg):
    B, T, W = q.shape
    n_chunks = tg // CHUNK
    row_spec = pl.BlockSpec((B, tg, W), lambda i: (0, i, 0))
    return pl.pallas_call(
        functools.partial(_gdn_kernel, n_chunks=n_chunks),
        grid=(T // tg,),
        in_specs=[row_spec, row_spec, row_spec, row_spec,
                  pl.BlockSpec((B, n_chunks, 2 * GDN_HEADS, CHUNK), lambda i: (0, i, 0, 0)),
                  _const_spec(norm_g.shape)],
        out_specs=row_spec,
        out_shape=jax.ShapeDtypeStruct((B, T, W), F32),
        scratch_shapes=[pltpu.VMEM((B * GDN_HEADS, GDN_HEAD_DIM, GDN_HEAD_DIM), F32)],
        compiler_params=_params(1),
        name="gdn",
    )(q, k, v, gate, gb, norm_g)


def _attn_kernel(x_ref, og_ref, c_ref, g_ref, wout_ref, kv_ref, wq_ref, wo_ref, h_ref):
    D = x_ref.shape[2]
    dh = D // XATTN_HEADS
    mix = jnp.concatenate([og_ref[0].astype(BF16), c_ref[0].astype(BF16)], axis=1)
    y = jnp.dot(mix, wout_ref[...], preferred_element_type=F32)
    h1 = x_ref[0] + _rms(y, g_ref[1:2, :])

    q = _mm(_rms(h1, g_ref[2:3, :]), wq_ref[...]) * (dh ** -0.5)
    heads = []
    for hd in range(XATTN_HEADS):
        kh = kv_ref[0, :, hd * dh:(hd + 1) * dh]
        vh = kv_ref[0, :, D + hd * dh:D + (hd + 1) * dh]
        s = _mm_nt(q[:, hd * dh:(hd + 1) * dh], kh)
        e = jnp.exp(s - jnp.max(s, axis=-1, keepdims=True))
        heads.append(_mm(e, vh) * (1.0 / jnp.sum(e, axis=-1, keepdims=True)))
    y = _mm(jnp.concatenate(heads, axis=1), wo_ref[...])
    h_ref[0] = h1 + _rms(y, g_ref[3:4, :])


def _attn(x, o_gdn, c, norm_g, w_out, kv, w_q, w_o, *, tm):
    B, T, D = x.shape
    M = kv.shape[1]
    row_spec = lambda w: pl.BlockSpec((1, tm, w), lambda b, i: (b, i, 0))
    return pl.pallas_call(
        _attn_kernel,
        grid=(B, T // tm),
        in_specs=[row_spec(D), row_spec(o_gdn.shape[2]), row_spec(c.shape[2]), _const_spec(norm_g.shape),
                  _const_spec(w_out.shape), pl.BlockSpec((1, M, 2 * D), lambda b, i: (b, 0, 0)),
                  _const_spec(w_q.shape), _const_spec(w_o.shape)],
        out_specs=row_spec(D),
        out_shape=jax.ShapeDtypeStruct((B, T, D), F32),
        compiler_params=_params(2),
        name="attn",
    )(x, o_gdn, c, norm_g, w_out, kv, w_q, w_o)


def _ffn_kernel(h_ref, g_ref, wup_ref, cw_ref, cb_ref, wdn_ref, o_ref, ubuf, carry, *, tm, fc, conv_k):
    F = wdn_ref.shape[0]
    halo = SUBLANES

    @pl.when(pl.program_id(1) == 0)
    def _():
        carry[...] = jnp.zeros(carry.shape, F32)

    h2 = h_ref[0]
    hn = _rms(h2, g_ref[4:5, :]).astype(BF16)

    def conv_cols(lo):
        ubuf[0:halo, :] = carry[:, lo:lo + fc]
        ubuf[halo:halo + tm, :] = jnp.dot(hn, wup_ref[:, lo:lo + fc], preferred_element_type=F32)
        carry[:, lo:lo + fc] = ubuf[tm:tm + halo, :]
        base = halo - (conv_k - 1)
        out = cb_ref[:, lo:lo + fc] + cw_ref[0:1, lo:lo + fc] * ubuf[base:base + tm, :]
        for j in range(1, conv_k):
            out = out + cw_ref[j:j + 1, lo:lo + fc] * ubuf[base + j:base + j + tm, :]
        return out

    acc = jnp.zeros((tm, o_ref.shape[2]), F32)
    for j in range(F // fc):
        a = conv_cols(j * fc)
        b = conv_cols(F + j * fc)
        acc = acc + jnp.dot((_silu(a) * b).astype(BF16), wdn_ref[j * fc:(j + 1) * fc, :],
                            preferred_element_type=F32)
    o_ref[0] = h2 + _rms(acc, g_ref[5:6, :])


def _ffn(h, norm_g, w_up, conv_w, conv_b, w_down, *, tm, fc):
    B, T, D = h.shape
    F = w_down.shape[0]
    row_spec = pl.BlockSpec((1, tm, D), lambda b, i: (b, i, 0))
    kernel = functools.partial(_ffn_kernel, tm=tm, fc=fc, conv_k=conv_w.shape[0])
    return pl.pallas_call(
        kernel,
        grid=(B, T // tm),
        in_specs=[row_spec, _const_spec(norm_g.shape), _const_spec(w_up.shape), _const_spec(conv_w.shape),
                  _const_spec(conv_b.shape), _const_spec(w_down.shape)],
        out_specs=row_spec,
        out_shape=jax.ShapeDtypeStruct((B, T, D), F32),
        scratch_shapes=[pltpu.VMEM((SUBLANES + tm, fc), F32), pltpu.VMEM((SUBLANES, 2 * F), F32)],
        compiler_params=_params(2),
        name="ffn",
    )(h, norm_g, w_up, conv_w, conv_b, w_down)


def _pick_tile(T, pref):
    t = min(T, pref)
    assert T % t == 0 and t % CHUNK == 0, (T, t)
    return t


def _layer(h, mem, g, w_in, gdn_conv_w, a_log, dt_bias, gdn_norm_g, cfm_dw_w, cfm_dw_b, cfm_ln_g, cfm_ln_b,
           w_out, mem_norm_g, xa_w_q, xa_w_kv, xa_w_o, ffn_w_up, ffn_conv_w, ffn_conv_b, ffn_w_down):
    B, T, D = h.shape
    s2 = 4 * GDN_WIDTH
    s4 = s2 + 2 * GDN_HEADS
    w_qkvg = w_in[:, :s2].astype(BF16)
    w_ab = w_in[:, s2:s4].T.astype(BF16)
    w_cfm = w_in[:, s4:].astype(BF16)
    pad = jnp.zeros((GDN_HEADS,), F32)
    a_log8 = jnp.concatenate([a_log.astype(F32), pad])[:, None]
    dt_bias8 = jnp.concatenate([dt_bias.astype(F32), pad])[:, None]

    tm = _pick_tile(T, 256)
    kv = _mem_kv(mem, mem_norm_g[None, :], xa_w_kv.astype(BF16))
    q, k, v, gate, gb, c = _mixer_in(h, g[0:1], w_qkvg, w_ab, w_cfm, gdn_conv_w, a_log8, dt_bias8,
                                     cfm_dw_w, cfm_dw_b[None, :], cfm_ln_g[None, :], cfm_ln_b[None, :], tm=tm)
    o_gdn = _gdn(q, k, v, gate, gb, gdn_norm_g[None, :], tg=_pick_tile(T, CHUNK))
    h = _attn(h, o_gdn, c, g, w_out.astype(BF16), kv, xa_w_q.astype(BF16), xa_w_o.astype(BF16), tm=tm)
    return _ffn(h, g, ffn_w_up.astype(BF16), ffn_conv_w, ffn_conv_b[None, :], ffn_w_down.astype(BF16),
                tm=tm, fc=256)


def kernel(x, mem, norm_g, w_in, gdn_conv_w, gdn_a_log, gdn_dt_bias, gdn_norm_g, cfm_dw_w, cfm_dw_b,
           cfm_ln_g, cfm_ln_b, w_out, mem_norm_g, xa_w_q, xa_w_kv, xa_w_o, ffn_w_up, ffn_conv_w,
           ffn_conv_b, ffn_w_down):
    h = x
    for l in range(norm_g.shape[0]):
        h = _layer(h, mem, norm_g[l], w_in[l], gdn_conv_w[l], gdn_a_log[l], gdn_dt_bias[l], gdn_norm_g[l],
                   cfm_dw_w[l], cfm_dw_b[l], cfm_ln_g[l], cfm_ln_b[l], w_out[l], mem_norm_g[l],
                   xa_w_q[l], xa_w_kv[l], xa_w_o[l], ffn_w_up[l], ffn_conv_w[l], ffn_conv_b[l], ffn_w_down[l])
    return h
```

```python
import functools

import jax
import jax.numpy as jnp
from jax import lax
from jax.experimental import pallas as pl
from jax.experimental.pallas import tpu as pltpu

EPS = 1e-6
GDN_HEADS = 4
GDN_HEAD_DIM = 128
GDN_WIDTH = GDN_HEADS * GDN_HEAD_DIM
GDN_SHORT_CONV = 4
CHUNK = 64
XATTN_HEADS = 4

F32 = jnp.float32
BF16 = jnp.bfloat16

SUBLANES = 8
VMEM_LIMIT_BYTES = 56 * 1024 * 1024

NT_DIMS = (((1,), (1,)), ((), ()))
TN_DIMS = (((0,), (0,)), ((), ()))


def _mm(a, b):
    return jnp.dot(a.astype(BF16), b.astype(BF16), preferred_element_type=F32)


def _mm_nt(a, b):
    return lax.dot_general(a.astype(BF16), b.astype(BF16), NT_DIMS, preferred_element_type=F32)


def _mm_tn(a, b):
    return lax.dot_general(a.astype(BF16), b.astype(BF16), TN_DIMS, preferred_element_type=F32)


def _rms(x, g):
    return x * lax.rsqrt(jnp.mean(x * x, axis=-1, keepdims=True) + EPS) * g


NEG_LOG2_E = -1.4426950408889634


def _sigmoid(x):
    return 1.0 / (1.0 + jnp.exp2(x * NEG_LOG2_E))


def _silu(x):
    return x * _sigmoid(x)


def _delay_rows(x, tail, s, sub):
    rows, width = x.shape
    groups = jnp.concatenate([tail, x], axis=0).reshape(rows // SUBLANES + 1, SUBLANES, width)
    rot = pltpu.roll(groups, s, 1)
    return jnp.where(sub < s, rot[:-1], rot[1:]).reshape(rows, width)


def _const_spec(shape):
    zeros = (0,) * len(shape)
    return pl.BlockSpec(shape, lambda *_: zeros)


def _params(n_axes):
    return pltpu.CompilerParams(dimension_semantics=("arbitrary",) * n_axes,
                                vmem_limit_bytes=VMEM_LIMIT_BYTES)


def _mem_kv_kernel(mem_ref, g_ref, w_ref, kv_ref):
    mn = _rms(mem_ref[0], g_ref[...])
    kv_ref[0] = _mm(mn, w_ref[...]).astype(BF16)


def _mem_kv(mem, g, w_kv):
    B, M, D = mem.shape
    N = w_kv.shape[1]
    return pl.pallas_call(
        _mem_kv_kernel,
        grid=(B,),
        in_specs=[pl.BlockSpec((1, M, D), lambda b: (b, 0, 0)), _const_spec((1, D)), _const_spec((D, N))],
        out_specs=pl.BlockSpec((1, M, N), lambda b: (b, 0, 0)),
        out_shape=jax.ShapeDtypeStruct((B, M, N), BF16),
        compiler_params=_params(1),
        name="mem_kv",
    )(mem, g, w_kv)


def _mixer_in_kernel(x_ref, g_ref, wqkvg_ref, wab_ref, wcfm_ref, convw_ref, alog_ref, dtb_ref,
                     dww_ref, dwb_ref, lng_ref, lnb_ref,
                     q_ref, k_ref, v_ref, gate_ref, gb_ref, c_ref,
                     ztail, ctail, cshift, *, tm, cfm_k):
    qkv_w = 3 * GDN_WIDTH
    cw = c_ref.shape[2]
    c_hist = cshift.shape[1] - tm
    n_copies = cshift.shape[0]

    @pl.when(pl.program_id(1) == 0)
    def _():
        ztail[...] = jnp.zeros(ztail.shape, F32)
        ctail[...] = jnp.zeros(ctail.shape, F32)
        cshift[:, 0:c_hist, :] = jnp.zeros((n_copies, c_hist, cw), F32)

    hb = _rms(x_ref[0], g_ref[...]).astype(BF16)

    z = jnp.dot(hb, wqkvg_ref[...], preferred_element_type=F32)
    ab = lax.dot_general(wab_ref[...], hb, NT_DIMS, preferred_element_type=F32)
    cz = jnp.dot(hb, wcfm_ref[...], preferred_element_type=F32)

    gate_ref[0] = _silu(z[:, qkv_w:])
    zq = z[:, :qkv_w]
    tail = ztail[...]
    ztail[...] = zq[tm - SUBLANES:, :]
    row8 = lax.broadcasted_iota(jnp.int32, (1, SUBLANES, qkv_w), 1)
    kc = GDN_SHORT_CONV
    acc = convw_ref[kc - 1:kc, :] * zq
    for s in range(1, kc):
        acc = acc + convw_ref[kc - 1 - s:kc - s, :] * _delay_rows(zq, tail, s, row8)
    qkv = _silu(acc)
    q_scale = GDN_HEAD_DIM ** -0.5
    for h in range(GDN_HEADS):
        lo = h * GDN_HEAD_DIM
        qh = qkv[:, lo:lo + GDN_HEAD_DIM]
        kh = qkv[:, GDN_WIDTH + lo:GDN_WIDTH + lo + GDN_HEAD_DIM]
        q_ref[0, :, lo:lo + GDN_HEAD_DIM] = qh * (
            lax.rsqrt(jnp.sum(qh * qh, axis=-1, keepdims=True) + EPS) * q_scale)
        k_ref[0, :, lo:lo + GDN_HEAD_DIM] = kh * lax.rsqrt(jnp.sum(kh * kh, axis=-1, keepdims=True) + EPS)
    v_ref[0] = qkv[:, 2 * GDN_WIDTH:]

    sp_in = ab + dtb_ref[...]
    softplus = jnp.maximum(sp_in, 0.0) + jnp.log(1.0 + jnp.exp(-jnp.abs(sp_in)))
    g = -jnp.exp(alog_ref[...]) * softplus
    row = lax.broadcasted_iota(jnp.int32, ab.shape, 0)
    gb = jnp.where(row < GDN_HEADS, g, _sigmoid(ab))
    for c in range(tm // CHUNK):
        gb_ref[0, c] = gb[:, c * CHUNK:(c + 1) * CHUNK]

    glu = cz[:, :cw] * _sigmoid(cz[:, cw:])
    tail = ctail[...]
    ctail[...] = glu[tm - SUBLANES:, :]
    row8 = lax.broadcasted_iota(jnp.int32, (1, SUBLANES, cw), 1)
    cshift[0, c_hist:c_hist + tm, :] = glu
    for r in range(1, n_copies):
        cshift[r, c_hist:c_hist + tm, :] = _delay_rows(glu, tail, r, row8)
    conv = dwb_ref[...]
    for d in range(cfm_k):
        off = c_hist - SUBLANES * (d // SUBLANES)
        conv = conv + dww_ref[cfm_k - 1 - d:cfm_k - d, :] * cshift[d % SUBLANES, off:off + tm, :]
    for r in range(n_copies):
        cshift[r, 0:c_hist, :] = cshift[r, tm:tm + c_hist, :]
    mu = jnp.mean(conv, axis=-1, keepdims=True)
    xc = conv - mu
    var = jnp.mean(xc * xc, axis=-1, keepdims=True)
    c_ref[0] = _silu(xc * lax.rsqrt(var + EPS) * lng_ref[...] + lnb_ref[...])


def _mixer_in(x, g0, w_qkvg, w_ab, w_cfm, conv_w, a_log8, dt_bias8, dw_w, dw_b, ln_g, ln_b, *, tm):
    B, T, D = x.shape
    cfm_k, cw = dw_w.shape
    c_hist = (cfm_k - 1) // SUBLANES * SUBLANES
    row_spec = lambda w: pl.BlockSpec((1, tm, w), lambda b, i: (b, i, 0))
    kernel = functools.partial(_mixer_in_kernel, tm=tm, cfm_k=cfm_k)
    wide = jax.ShapeDtypeStruct((B, T, GDN_WIDTH), F32)
    return pl.pallas_call(
        kernel,
        grid=(B, T // tm),
        in_specs=[row_spec(D), _const_spec(g0.shape), _const_spec(w_qkvg.shape), _const_spec(w_ab.shape),
                  _const_spec(w_cfm.shape), _const_spec(conv_w.shape), _const_spec(a_log8.shape),
                  _const_spec(dt_bias8.shape), _const_spec(dw_w.shape), _const_spec(dw_b.shape),
                  _const_spec(ln_g.shape), _const_spec(ln_b.shape)],
        out_specs=[row_spec(GDN_WIDTH), row_spec(GDN_WIDTH), row_spec(GDN_WIDTH), row_spec(GDN_WIDTH),
                   pl.BlockSpec((1, tm // CHUNK, 2 * GDN_HEADS, CHUNK), lambda b, i: (b, i, 0, 0)),
                   row_spec(cw)],
        out_shape=[wide, wide, wide, wide,
                   jax.ShapeDtypeStruct((B, T // CHUNK, 2 * GDN_HEADS, CHUNK), F32),
                   jax.ShapeDtypeStruct((B, T, cw), F32)],
        scratch_shapes=[pltpu.VMEM((SUBLANES, 3 * GDN_WIDTH), F32),
                        pltpu.VMEM((SUBLANES, cw), F32),
                        pltpu.VMEM((min(SUBLANES, cfm_k), c_hist + tm, cw), F32)],
        compiler_params=_params(2),
        name="mixer_in",
    )(x, g0, w_qkvg, w_ab, w_cfm, conv_w, a_log8, dt_bias8, dw_w, dw_b, ln_g, ln_b)


def _unit_lower_inverse(Ls, eye, diag_mask, off_masks):
    Ps = [jnp.where(diag_mask, L, 0.0) for L in Ls]
    Ts = [eye - P for P in Ps]
    for _ in range(3):
        Ps = [_mm(P, P) for P in Ps]
        Ts = [T + _mm(T, P) for T, P in zip(Ts, Ps)]
    for m in off_masks:
        Xs = [_mm(T, jnp.where(m, L, 0.0)) for T, L in zip(Ts, Ls)]
        Ts = [T - _mm(X, T) for T, X in zip(Ts, Xs)]
    return Ts


def _gdn_kernel(q_ref, k_ref, v_ref, gate_ref, gb_ref, ng_ref, o_ref, s_ref, *, n_chunks):
    @pl.when(pl.program_id(0) == 0)
    def _():
        s_ref[...] = jnp.zeros(s_ref.shape, F32)

    C, H, Dh = CHUNK, GDN_HEADS, GDN_HEAD_DIM
    nb = q_ref.shape[0]
    units = [(b, h) for b in range(nb) for h in range(H)]
    ri = lax.broadcasted_iota(jnp.int32, (C, C), 0)
    ci = lax.broadcasted_iota(jnp.int32, (C, C), 1)
    causal = ri >= ci
    strict = ri > ci
    lower1 = jnp.where(causal, 1.0, 0.0).astype(F32)
    upper1 = jnp.where(ri <= ci, 1.0, 0.0).astype(F32)
    eye = jnp.where(ri == ci, 1.0, 0.0).astype(F32)
    diag_mask = (ri // 16) == (ci // 16)
    off_masks = [((ri // (2 * s)) == (ci // (2 * s))) & ((ri // s) > (ci // s)) for s in (16, 32)]
    hi = lax.Precision.HIGHEST
    ng = ng_ref[...]

    for c in range(n_chunks):
        rows = slice(c * C, (c + 1) * C)
        gc_row, gc_col, gb_col = [], [], []
        for b in range(nb):
            gbc = gb_ref[b, c]
            gc_row.append(jnp.dot(gbc, upper1, precision=hi, preferred_element_type=F32))
            gc_col.append(lax.dot_general(lower1, gbc, NT_DIMS, precision=hi, preferred_element_type=F32))
            gb_col.append(lax.dot_general(eye, gbc, NT_DIMS, precision=hi, preferred_element_type=F32))

        qgs, ks, kbqs, rhs, decays, kdecs, elast = [], [], [], [], [], [], []
        for b, h in units:
            cols = slice(h * Dh, (h + 1) * Dh)
            qh = q_ref[b, rows, cols]
            kh = k_ref[b, rows, cols]
            vh = v_ref[b, rows, cols]
            gcol = gc_col[b][:, h:h + 1]
            grow = gc_row[b][h:h + 1, :]
            bcol = gb_col[b][:, H + h:H + h + 1]
            g_last = grow[:, C - 1:C]
            egc = jnp.exp(gcol)
            kb = kh * bcol
            decays.append(jnp.where(causal, jnp.exp(jnp.where(causal, gcol - grow, 0.0)), 0.0))
            qgs.append(qh * egc)
            ks.append(kh.astype(BF16))
            kbqs.append(jnp.concatenate([kb, qh], axis=0))
            rhs.append(jnp.concatenate([vh * bcol, kb * egc], axis=1))
            kdecs.append(kh * jnp.exp(g_last - gcol))
            elast.append(jnp.exp(g_last))
        prods = [_mm_nt(a, k) for a, k in zip(kbqs, ks)]
        Ls = [jnp.where(strict, p[:C] * d, 0.0) for p, d in zip(prods, decays)]
        attns = [p[C:] * d for p, d in zip(prods, decays)]
        Ts = _unit_lower_inverse(Ls, eye, diag_mask, off_masks)
        sols = [_mm(T, r) for T, r in zip(Ts, rhs)]

        Ss = [s_ref[i] for i in range(len(units))]
        ws_qs = [_mm(jnp.concatenate([sol[:, Dh:], qg], axis=0), S) for sol, qg, S in zip(sols, qgs, Ss)]
        v_new = [sol[:, :Dh] - x[:C] for sol, x in zip(sols, ws_qs)]
        v_new_b = [x.astype(BF16) for x in v_new]
        outs = [x[C:] + _mm(a, vb) for x, a, vb in zip(ws_qs, attns, v_new_b)]
        for i, (S, e, kd, vb) in enumerate(zip(Ss, elast, kdecs, v_new_b)):
            s_ref[i] = S * e + _mm_tn(kd, vb)
        for (b, h), o in zip(units, outs):
            cols = slice(h * Dh, (h + 1) * Dh)
            o_ref[b, rows, cols] = _rms(o, ng) * gate_ref[b, rows, cols]


def _gdn(q, k, v, gate, gb, norm_g, *, tg):
    B, T, W = q.shape
    n_chunks = tg // CHUNK
    row_spec = pl.BlockSpec((B, tg, W), lambda i: (0, i, 0))
    return pl.pallas_call(
        functools.partial(_gdn_kernel, n_chunks=n_chunks),
        grid=(T // tg,),
        in_specs=[row_spec, row_spec, row_spec, row_spec,
                  pl.BlockSpec((B, n_chunks, 2 * GDN_HEADS, CHUNK), lambda i: (0, i, 0, 0)),
                  _const_spec(norm_g.shape)],
        out_specs=row_spec,
        out_shape=jax.ShapeDtypeStruct((B, T, W), F32),
        scratch_shapes=[pltpu.VMEM((B * GDN_HEADS, GDN_HEAD_DIM, GDN_HEAD_DIM), F32)],
        compiler_params=_params(1),
        name="gdn",
    )(q, k, v, gate, gb, norm_g)


def _attn_kernel(x_ref, og_ref, c_ref, g_ref, wout_ref, kv_ref, wq_ref, wo_ref, h_ref):
    D = x_ref.shape[2]
    dh = D // XATTN_HEADS
    mix = jnp.concatenate([og_ref[0].astype(BF16), c_ref[0].astype(BF16)], axis=1)
    y = jnp.dot(mix, wout_ref[...], preferred_element_type=F32)
    h1 = x_ref[0] + _rms(y, g_ref[1:2, :])

    q = _mm(_rms(h1, g_ref[2:3, :]), wq_ref[...]) * (dh ** -0.5)
    heads = []
    for hd in range(XATTN_HEADS):
        kh = kv_ref[0, :, hd * dh:(hd + 1) * dh]
        vh = kv_ref[0, :, D + hd * dh:D + (hd + 1) * dh]
        s = _mm_nt(q[:, hd * dh:(hd + 1) * dh], kh)
        e = jnp.exp(s - jnp.max(s, axis=-1, keepdims=True))
        heads.append(_mm(e, vh) * (1.0 / jnp.sum(e, axis=-1, keepdims=True)))
    y = _mm(jnp.concatenate(heads, axis=1), wo_ref[...])
    h_ref[0] = h1 + _rms(y, g_ref[3:4, :])


def _attn(x, o_gdn, c, norm_g, w_out, kv, w_q, w_o, *, tm):
    B, T, D = x.shape
    M = kv.shape[1]
    row_spec = lambda w: pl.BlockSpec((1, tm, w), lambda b, i: (b, i, 0))
    return pl.pallas_call(
        _attn_kernel,
        grid=(B, T // tm),
        in_specs=[row_spec(D), row_spec(o_gdn.shape[2]), row_spec(c.shape[2]), _const_spec(norm_g.shape),
                  _const_spec(w_out.shape), pl.BlockSpec((1, M, 2 * D), lambda b, i: (b, 0, 0)),
                  _const_spec(w_q.shape), _const_spec(w_o.shape)],
        out_specs=row_spec(D),
        out_shape=jax.ShapeDtypeStruct((B, T, D), F32),
        compiler_params=_params(2),
        name="attn",
    )(x, o_gdn, c, norm_g, w_out, kv, w_q, w_o)


def _ffn_kernel(h_ref, g_ref, wup_ref, cw_ref, cb_ref, wdn_ref, o_ref, carry, *, tm, fc, conv_k):
    F = wdn_ref.shape[0]
    n_chunks = F // fc
    rb = 32

    @pl.when(pl.program_id(1) == 0)
    def _():
        carry[...] = jnp.zeros(carry.shape, F32)

    h2 = h_ref[0]
    hn = _rms(h2, g_ref[4:5, :]).astype(BF16)
    row8 = lax.broadcasted_iota(jnp.int32, (1, SUBLANES, fc), 1)

    def up(j):
        return [jnp.dot(hn, wup_ref[:, lo:lo + fc], preferred_element_type=F32) for lo in (j * fc, F + j * fc)]

    def conv(u, lo, r0):
        x = u[r0:r0 + rb, :]
        tail = carry[:, lo:lo + fc] if r0 == 0 else u[r0 - SUBLANES:r0, :]
        out = cb_ref[:, lo:lo + fc] + cw_ref[conv_k - 1:conv_k, lo:lo + fc] * x
        for s in range(1, conv_k):
            out = out + cw_ref[conv_k - 1 - s:conv_k - s, lo:lo + fc] * _delay_rows(x, tail, s, row8)
        return out

    acc = None
    nxt = up(0)
    for j in range(n_chunks):
        ua, ub = nxt
        if j + 1 < n_chunks:
            nxt = up(j + 1)
        acts = [(_silu(conv(ua, j * fc, r0)) * conv(ub, F + j * fc, r0)).astype(BF16) for r0 in range(0, tm, rb)]
        carry[:, j * fc:(j + 1) * fc] = ua[tm - SUBLANES:, :]
        carry[:, F + j * fc:F + (j + 1) * fc] = ub[tm - SUBLANES:, :]
        d = jnp.dot(jnp.concatenate(acts, axis=0), wdn_ref[j * fc:(j + 1) * fc, :], preferred_element_type=F32)
        acc = d if acc is None else acc + d
    o_ref[0] = h2 + _rms(acc, g_ref[5:6, :])


def _ffn(h, norm_g, w_up, conv_w, conv_b, w_down, *, tm, fc):
    B, T, D = h.shape
    F = w_down.shape[0]
    row_spec = pl.BlockSpec((1, tm, D), lambda b, i: (b, i, 0))
    kernel = functools.partial(_ffn_kernel, tm=tm, fc=fc, conv_k=conv_w.shape[0])
    return pl.pallas_call(
        kernel,
        grid=(B, T // tm),
        in_specs=[row_spec, _const_spec(norm_g.shape), _const_spec(w_up.shape), _const_spec(conv_w.shape),
                  _const_spec(conv_b.shape), _const_spec(w_down.shape)],
        out_specs=row_spec,
        out_shape=jax.ShapeDtypeStruct((B, T, D), F32),
        scratch_shapes=[pltpu.VMEM((SUBLANES, 2 * F), F32)],
        compiler_params=_params(2),
        name="ffn",
    )(h, norm_g, w_up, conv_w, conv_b, w_down)


def _pick_tile(T, pref):
    t = min(T, pref)
    assert T % t == 0 and t % CHUNK == 0, (T, t)
    return t


def _layer(h, mem, g, w_in, gdn_conv_w, a_log, dt_bias, gdn_norm_g, cfm_dw_w, cfm_dw_b, cfm_ln_g, cfm_ln_b,
           w_out, mem_norm_g, xa_w_q, xa_w_kv, xa_w_o, ffn_w_up, ffn_conv_w, ffn_conv_b, ffn_w_down):
    B, T, D = h.shape
    s2 = 4 * GDN_WIDTH
    s4 = s2 + 2 * GDN_HEADS
    w_qkvg = w_in[:, :s2].astype(BF16)
    w_ab = w_in[:, s2:s4].T.astype(BF16)
    w_cfm = w_in[:, s4:].astype(BF16)
    pad = jnp.zeros((GDN_HEADS,), F32)
    a_log8 = jnp.concatenate([a_log.astype(F32), pad])[:, None]
    dt_bias8 = jnp.concatenate([dt_bias.astype(F32), pad])[:, None]

    tm = _pick_tile(T, 512)
    kv = _mem_kv(mem, mem_norm_g[None, :], xa_w_kv.astype(BF16))
    q, k, v, gate, gb, c = _mixer_in(h, g[0:1], w_qkvg, w_ab, w_cfm, gdn_conv_w, a_log8, dt_bias8,
                                     cfm_dw_w, cfm_dw_b[None, :], cfm_ln_g[None, :], cfm_ln_b[None, :], tm=tm)
    o_gdn = _gdn(q, k, v, gate, gb, gdn_norm_g[None, :], tg=_pick_tile(T, CHUNK))
    h = _attn(h, o_gdn, c, g, w_out.astype(BF16), kv, xa_w_q.astype(BF16), xa_w_o.astype(BF16), tm=tm)
    return _ffn(h, g, ffn_w_up.astype(BF16), ffn_conv_w, ffn_conv_b[None, :], ffn_w_down.astype(BF16),
                tm=tm, fc=256)


def kernel(x, mem, norm_g, w_in, gdn_conv_w, gdn_a_log, gdn_dt_bias, gdn_norm_g, cfm_dw_w, cfm_dw_b,
           cfm_ln_g, cfm_ln_b, w_out, mem_norm_g, xa_w_q, xa_w_kv, xa_w_o, ffn_w_up, ffn_conv_w,
           ffn_conv_b, ffn_w_down):
    h = x
    for l in range(norm_g.shape[0]):
        h = _layer(h, mem, norm_g[l], w_in[l], gdn_conv_w[l], gdn_a_log[l], gdn_dt_bias[l], gdn_norm_g[l],
                   cfm_dw_w[l], cfm_dw_b[l], cfm_ln_g[l], cfm_ln_b[l], w_out[l], mem_norm_g[l],
                   xa_w_q[l], xa_w_kv[l], xa_w_o[l], ffn_w_up[l], ffn_conv_w[l], ffn_conv_b[l], ffn_w_down[l])
    return h
```

```python
import functools

import jax
import jax.numpy as jnp
from jax import lax
from jax.experimental import pallas as pl
from jax.experimental.pallas import tpu as pltpu

EPS = 1e-6
GDN_HEADS = 4
GDN_HEAD_DIM = 128
GDN_WIDTH = GDN_HEADS * GDN_HEAD_DIM
GDN_SHORT_CONV = 4
CHUNK = 64
XATTN_HEADS = 4

F32 = jnp.float32
BF16 = jnp.bfloat16

SUBLANES = 8
PHASES = SUBLANES
VMEM_LIMIT_BYTES = 56 * 1024 * 1024

NT_DIMS = (((1,), (1,)), ((), ()))
TN_DIMS = (((0,), (0,)), ((), ()))


def _mm(a, b):
    return jnp.dot(a.astype(BF16), b.astype(BF16), preferred_element_type=F32)


def _mm_nt(a, b):
    return lax.dot_general(a.astype(BF16), b.astype(BF16), NT_DIMS, preferred_element_type=F32)


def _mm_tn(a, b):
    return lax.dot_general(a.astype(BF16), b.astype(BF16), TN_DIMS, preferred_element_type=F32)


def _rms(x, g):
    return x * lax.rsqrt(jnp.mean(x * x, axis=-1, keepdims=True) + EPS) * g


NEG_LOG2_E = -1.4426950408889634


def _sigmoid(x):
    return 1.0 / (1.0 + jnp.exp2(x * NEG_LOG2_E))


def _silu(x):
    return x * _sigmoid(x)


def _delay_rows(x, tail, s, sub):
    rows, width = x.shape
    groups = jnp.concatenate([tail, x], axis=0).reshape(rows // SUBLANES + 1, SUBLANES, width)
    rot = pltpu.roll(groups, s, 1)
    return jnp.where(sub < s, rot[:-1], rot[1:]).reshape(rows, width)


def _phase_slabs(ref, width):
    return [ref[0, :, v * width:(v + 1) * width] for v in range(PHASES)]


def _phase_conv(slabs, tails, taps, sub, bias=None):
    K = len(taps)
    delayed = {w: _delay_rows(slabs[w], tails[w], 1, sub) for w in range(PHASES - (K - 1), PHASES)}
    outs = []
    for v in range(PHASES):
        acc = taps[K - 1] * slabs[v]
        if bias is not None:
            acc = acc + bias
        for d in range(1, K):
            acc = acc + taps[K - 1 - d] * (slabs[v - d] if v >= d else delayed[v - d + PHASES])
        outs.append(acc)
    return outs


def _const_spec(shape):
    zeros = (0,) * len(shape)
    return pl.BlockSpec(shape, lambda *_: zeros)


def _params(n_axes):
    return pltpu.CompilerParams(dimension_semantics=("arbitrary",) * n_axes,
                                vmem_limit_bytes=VMEM_LIMIT_BYTES)


def _mem_kv_kernel(mem_ref, g_ref, w_ref, kv_ref):
    mn = _rms(mem_ref[0], g_ref[...])
    kv_ref[0] = _mm(mn, w_ref[...]).astype(BF16)


def _mem_kv(mem, g, w_kv):
    B, M, D = mem.shape
    N = w_kv.shape[1]
    return pl.pallas_call(
        _mem_kv_kernel,
        grid=(B,),
        in_specs=[pl.BlockSpec((1, M, D), lambda b: (b, 0, 0)), _const_spec((1, D)), _const_spec((D, N))],
        out_specs=pl.BlockSpec((1, M, N), lambda b: (b, 0, 0)),
        out_shape=jax.ShapeDtypeStruct((B, M, N), BF16),
        compiler_params=_params(1),
        name="mem_kv",
    )(mem, g, w_kv)


def _mixer_in_kernel(x_ref, g_ref, wqkvg_ref, wab_ref, wcfm_ref, convw_ref, alog_ref, dtb_ref,
                     dww_ref, dwb_ref, lng_ref, lnb_ref,
                     q_ref, k_ref, v_ref, gate_ref, gb_ref, c_ref,
                     ztail, ctail, cdel, perm, *, tm, cfm_k):
    qkv_w = 3 * GDN_WIDTH
    W = GDN_WIDTH
    D = x_ref.shape[2] // PHASES
    cw = c_ref.shape[2] // PHASES
    R = tm // PHASES
    kc = GDN_SHORT_CONV
    first_tail = PHASES - (kc - 1)

    @pl.when(pl.program_id(1) == 0)
    def _():
        ztail[...] = jnp.zeros(ztail.shape, F32)
        ctail[...] = jnp.zeros(ctail.shape, F32)

    @pl.when((pl.program_id(0) == 0) & (pl.program_id(1) == 0))
    def _():
        src = lax.broadcasted_iota(jnp.int32, (tm, tm), 0)
        dst = lax.broadcasted_iota(jnp.int32, (tm, tm), 1)
        g_, v_, s_ = dst // CHUNK, (dst % CHUNK) // SUBLANES, dst % SUBLANES
        perm[...] = jnp.where(src == v_ * R + g_ * SUBLANES + s_, 1.0, 0.0).astype(F32)

    h = jnp.concatenate(_phase_slabs(x_ref, D), axis=0)
    hb = _rms(h, g_ref[...]).astype(BF16)

    z = jnp.dot(hb, wqkvg_ref[...], preferred_element_type=F32)
    ab = lax.dot_general(wab_ref[...], hb, NT_DIMS, preferred_element_type=F32)
    cz = jnp.dot(hb, wcfm_ref[...], preferred_element_type=F32)

    sub_z = lax.broadcasted_iota(jnp.int32, (1, SUBLANES, qkv_w), 1)
    zs = [z[v * R:(v + 1) * R, :qkv_w] for v in range(PHASES)]
    tails = {w: ztail[w - first_tail] for w in range(first_tail, PHASES)}
    for w in range(first_tail, PHASES):
        ztail[w - first_tail] = zs[w][R - SUBLANES:, :]
    taps = [convw_ref[j:j + 1, :] for j in range(kc)]
    q_scale = GDN_HEAD_DIM ** -0.5
    for v, acc in enumerate(_phase_conv(zs, tails, taps, sub_z)):
        qkv = _silu(acc)
        gate_ref[0, :, v * W:(v + 1) * W] = _silu(z[v * R:(v + 1) * R, qkv_w:])
        for hd in range(GDN_HEADS):
            lo = hd * GDN_HEAD_DIM
            qh = qkv[:, lo:lo + GDN_HEAD_DIM]
            kh = qkv[:, W + lo:W + lo + GDN_HEAD_DIM]
            q_ref[0, :, v * W + lo:v * W + lo + GDN_HEAD_DIM] = qh * (
                lax.rsqrt(jnp.sum(qh * qh, axis=-1, keepdims=True) + EPS) * q_scale)
            k_ref[0, :, v * W + lo:v * W + lo + GDN_HEAD_DIM] = kh * lax.rsqrt(
                jnp.sum(kh * kh, axis=-1, keepdims=True) + EPS)
        v_ref[0, :, v * W:(v + 1) * W] = qkv[:, 2 * W:]

    ab = jnp.dot(ab, perm[...], precision=lax.Precision.HIGHEST, preferred_element_type=F32)
    sp_in = ab + dtb_ref[...]
    softplus = jnp.maximum(sp_in, 0.0) + jnp.log(1.0 + jnp.exp(-jnp.abs(sp_in)))
    g = -jnp.exp(alog_ref[...]) * softplus
    row = lax.broadcasted_iota(jnp.int32, ab.shape, 0)
    gb = jnp.where(row < GDN_HEADS, g, _sigmoid(ab))
    for c in range(tm // CHUNK):
        gb_ref[0, c] = gb[:, c * CHUNK:(c + 1) * CHUNK]

    sub_c = lax.broadcasted_iota(jnp.int32, (1, SUBLANES, cw), 1)
    n_del = cdel.shape[0]
    for w in range(PHASES):
        glu = cz[w * R:(w + 1) * R, :cw] * _sigmoid(cz[w * R:(w + 1) * R, cw:])
        tail = ctail[w]
        ctail[w] = glu[R - SUBLANES:, :]
        cdel[0, w * R:(w + 1) * R, :] = glu
        for m in range(1, n_del):
            cdel[m, w * R:(w + 1) * R, :] = _delay_rows(glu, tail, m, sub_c)
    convs = []
    for v in range(PHASES):
        acc = dwb_ref[...]
        for d in range(cfm_k):
            a, e = divmod(d, PHASES)
            w, m = (v - e) % PHASES, a + (1 if v < e else 0)
            acc = acc + dww_ref[cfm_k - 1 - d:cfm_k - d, :] * cdel[m, w * R:(w + 1) * R, :]
        convs.append(acc)
    conv = jnp.concatenate(convs, axis=0)
    mu = jnp.mean(conv, axis=-1, keepdims=True)
    xc = conv - mu
    var = jnp.mean(xc * xc, axis=-1, keepdims=True)
    cout = _silu(xc * lax.rsqrt(var + EPS) * lng_ref[...] + lnb_ref[...])
    for v in range(PHASES):
        c_ref[0, :, v * cw:(v + 1) * cw] = cout[v * R:(v + 1) * R, :]


def _mixer_in(x, g0, w_qkvg, w_ab, w_cfm, conv_w, a_log8, dt_bias8, dw_w, dw_b, ln_g, ln_b, *, tm):
    B, T, D = x.shape
    cfm_k, cw = dw_w.shape
    assert GDN_SHORT_CONV - 1 < PHASES and tm % (PHASES * SUBLANES) == 0
    n_del = (cfm_k - 1) // PHASES + 2
    assert n_del - 1 < SUBLANES
    R = tm // PHASES
    row_spec = lambda w: pl.BlockSpec((1, R, PHASES * w), lambda b, i: (b, i, 0))
    kernel = functools.partial(_mixer_in_kernel, tm=tm, cfm_k=cfm_k)
    wide = jax.ShapeDtypeStruct((B, T // PHASES, PHASES * GDN_WIDTH), F32)
    q, k, v, gate, gb, c = pl.pallas_call(
        kernel,
        grid=(B, T // tm),
        in_specs=[row_spec(D), _const_spec(g0.shape), _const_spec(w_qkvg.shape), _const_spec(w_ab.shape),
                  _const_spec(w_cfm.shape), _const_spec(conv_w.shape), _const_spec(a_log8.shape),
                  _const_spec(dt_bias8.shape), _const_spec(dw_w.shape), _const_spec(dw_b.shape),
                  _const_spec(ln_g.shape), _const_spec(ln_b.shape)],
        out_specs=[row_spec(GDN_WIDTH), row_spec(GDN_WIDTH), row_spec(GDN_WIDTH), row_spec(GDN_WIDTH),
                   pl.BlockSpec((1, tm // CHUNK, 2 * GDN_HEADS, CHUNK), lambda b, i: (b, i, 0, 0)),
                   row_spec(cw)],
        out_shape=[wide, wide, wide, wide,
                   jax.ShapeDtypeStruct((B, T // CHUNK, 2 * GDN_HEADS, CHUNK), F32),
                   jax.ShapeDtypeStruct((B, T // PHASES, PHASES * cw), F32)],
        scratch_shapes=[pltpu.VMEM((GDN_SHORT_CONV - 1, SUBLANES, 3 * GDN_WIDTH), F32),
                        pltpu.VMEM((PHASES, SUBLANES, cw), F32),
                        pltpu.VMEM((n_del, tm, cw), F32),
                        pltpu.VMEM((tm, tm), F32)],
        compiler_params=_params(2),
        name="mixer_in",
    )(x.reshape(B, T // PHASES, PHASES * D), g0, w_qkvg, w_ab, w_cfm, conv_w, a_log8, dt_bias8,
      dw_w, dw_b, ln_g, ln_b)
    return q, k, v, gate, gb, c.reshape(B, T, cw)


def _unit_lower_inverse(Ls, eye, diag_mask, off_masks):
    Ps = [jnp.where(diag_mask, L, 0.0) for L in Ls]
    Ts = [eye - P for P in Ps]
    for _ in range(3):
        Ps = [_mm(P, P) for P in Ps]
        Ts = [T + _mm(T, P) for T, P in zip(Ts, Ps)]
    for m in off_masks:
        Xs = [_mm(T, jnp.where(m, L, 0.0)) for T, L in zip(Ts, Ls)]
        Ts = [T - _mm(X, T) for T, X in zip(Ts, Xs)]
    return Ts


def _gdn_kernel(q_ref, k_ref, v_ref, gate_ref, gb_ref, ng_ref, o_ref, s_ref, *, n_chunks):
    @pl.when(pl.program_id(0) == 0)
    def _():
        s_ref[...] = jnp.zeros(s_ref.shape, F32)

    C, H, Dh = CHUNK, GDN_HEADS, GDN_HEAD_DIM
    W = H * Dh
    rows_per_chunk = C // PHASES
    nb = q_ref.shape[0]
    units = [(b, h) for b in range(nb) for h in range(H)]
    ri = lax.broadcasted_iota(jnp.int32, (C, C), 0)
    ci = lax.broadcasted_iota(jnp.int32, (C, C), 1)
    ti = PHASES * (ri % rows_per_chunk) + ri // rows_per_chunk
    tj = PHASES * (ci % rows_per_chunk) + ci // rows_per_chunk
    causal = ti >= tj
    strict = ti > tj
    lower1 = jnp.where(causal, 1.0, 0.0).astype(F32)
    upper1 = jnp.where(ti <= tj, 1.0, 0.0).astype(F32)
    eye = jnp.where(ri == ci, 1.0, 0.0).astype(F32)
    diag_mask = (ti // 16) == (tj // 16)
    off_masks = [((ti // (2 * s)) == (tj // (2 * s))) & ((ti // s) > (tj // s)) for s in (16, 32)]
    hi = lax.Precision.HIGHEST
    ng = ng_ref[...]

    def load(ref, b, c, h):
        r0 = c * rows_per_chunk
        return jnp.concatenate([ref[b, r0:r0 + rows_per_chunk, p * W + h * Dh:p * W + (h + 1) * Dh]
                                for p in range(PHASES)], axis=0)

    for c in range(n_chunks):
        gc_row, gc_col, gb_col = [], [], []
        for b in range(nb):
            gbc = gb_ref[b, c]
            gc_row.append(jnp.dot(gbc, upper1, precision=hi, preferred_element_type=F32))
            gc_col.append(lax.dot_general(lower1, gbc, NT_DIMS, precision=hi, preferred_element_type=F32))
            gb_col.append(lax.dot_general(eye, gbc, NT_DIMS, precision=hi, preferred_element_type=F32))

        qgs, ks, kbqs, rhs, decays, kdecs, elast = [], [], [], [], [], [], []
        for b, h in units:
            qh = load(q_ref, b, c, h)
            kh = load(k_ref, b, c, h)
            vh = load(v_ref, b, c, h)
            gcol = gc_col[b][:, h:h + 1]
            grow = gc_row[b][h:h + 1, :]
            bcol = gb_col[b][:, H + h:H + h + 1]
            g_last = grow[:, C - 1:C]
            egc = jnp.exp(gcol)
            kb = kh * bcol
            decays.append(jnp.where(causal, jnp.exp(jnp.where(causal, gcol - grow, 0.0)), 0.0))
            qgs.append(qh * egc)
            ks.append(kh.astype(BF16))
            kbqs.append(jnp.concatenate([kb, qh], axis=0))
            rhs.append(jnp.concatenate([vh * bcol, kb * egc], axis=1))
            kdecs.append(kh * jnp.exp(g_last - gcol))
            elast.append(jnp.exp(g_last))
        prods = [_mm_nt(a, k) for a, k in zip(kbqs, ks)]
        Ls = [jnp.where(strict, p[:C] * d, 0.0) for p, d in zip(prods, decays)]
        attns = [p[C:] * d for p, d in zip(prods, decays)]
        Ts = _unit_lower_inverse(Ls, eye, diag_mask, off_masks)
        sols = [_mm(T, r) for T, r in zip(Ts, rhs)]

        Ss = [s_ref[i] for i in range(len(units))]
        ws_qs = [_mm(jnp.concatenate([sol[:, Dh:], qg], axis=0), S) for sol, qg, S in zip(sols, qgs, Ss)]
        v_new = [sol[:, :Dh] - x[:C] for sol, x in zip(sols, ws_qs)]
        v_new_b = [x.astype(BF16) for x in v_new]
        outs = [x[C:] + _mm(a, vb) for x, a, vb in zip(ws_qs, attns, v_new_b)]
        for i, (S, e, kd, vb) in enumerate(zip(Ss, elast, kdecs, v_new_b)):
            s_ref[i] = S * e + _mm_tn(kd, vb)
        r0 = c * rows_per_chunk
        for (b, h), o in zip(units, outs):
            on = _rms(o, ng) * load(gate_ref, b, c, h)
            for p in range(PHASES):
                o_ref[b, r0:r0 + rows_per_chunk, p * W + h * Dh:p * W + (h + 1) * Dh] = (
                    on[p * rows_per_chunk:(p + 1) * rows_per_chunk])


def _gdn(q, k, v, gate, gb, norm_g, *, tg):
    B, TP, WP = q.shape
    n_chunks = tg // CHUNK
    row_spec = pl.BlockSpec((B, tg // PHASES, WP), lambda i: (0, i, 0))
    return pl.pallas_call(
        functools.partial(_gdn_kernel, n_chunks=n_chunks),
        grid=(TP * PHASES // tg,),
        in_specs=[row_spec, row_spec, row_spec, row_spec,
                  pl.BlockSpec((B, n_chunks, 2 * GDN_HEADS, CHUNK), lambda i: (0, i, 0, 0)),
                  _const_spec(norm_g.shape)],
        out_specs=row_spec,
        out_shape=jax.ShapeDtypeStruct((B, TP, WP), F32),
        scratch_shapes=[pltpu.VMEM((B * GDN_HEADS, GDN_HEAD_DIM, GDN_HEAD_DIM), F32)],
        compiler_params=_params(1),
        name="gdn",
    )(q, k, v, gate, gb, norm_g)


def _attn_kernel(x_ref, og_ref, c_ref, g_ref, wout_ref, kv_ref, wq_ref, wo_ref, h_ref):
    D = x_ref.shape[2]
    dh = D // XATTN_HEADS
    mix = jnp.concatenate([og_ref[0].astype(BF16), c_ref[0].astype(BF16)], axis=1)
    y = jnp.dot(mix, wout_ref[...], preferred_element_type=F32)
    h1 = x_ref[0] + _rms(y, g_ref[1:2, :])

    q = _mm(_rms(h1, g_ref[2:3, :]), wq_ref[...]) * (dh ** -0.5)
    heads = []
    for hd in range(XATTN_HEADS):
        kh = kv_ref[0, :, hd * dh:(hd + 1) * dh]
        vh = kv_ref[0, :, D + hd * dh:D + (hd + 1) * dh]
        s = _mm_nt(q[:, hd * dh:(hd + 1) * dh], kh)
        e = jnp.exp(s - jnp.max(s, axis=-1, keepdims=True))
        heads.append(_mm(e, vh) * (1.0 / jnp.sum(e, axis=-1, keepdims=True)))
    y = _mm(jnp.concatenate(heads, axis=1), wo_ref[...])
    h_ref[0] = h1 + _rms(y, g_ref[3:4, :])


def _attn(x, o_gdn, c, norm_g, w_out, kv, w_q, w_o, *, tm):
    B, T, D = x.shape
    M = kv.shape[1]
    row_spec = lambda w: pl.BlockSpec((1, tm, w), lambda b, i: (b, i, 0))
    return pl.pallas_call(
        _attn_kernel,
        grid=(B, T // tm),
        in_specs=[row_spec(D), row_spec(o_gdn.shape[2]), row_spec(c.shape[2]), _const_spec(norm_g.shape),
                  _const_spec(w_out.shape), pl.BlockSpec((1, M, 2 * D), lambda b, i: (b, 0, 0)),
                  _const_spec(w_q.shape), _const_spec(w_o.shape)],
        out_specs=row_spec(D),
        out_shape=jax.ShapeDtypeStruct((B, T, D), F32),
        compiler_params=_params(2),
        name="attn",
    )(x, o_gdn, c, norm_g, w_out, kv, w_q, w_o)


def _ffn_kernel(h_ref, g_ref, wup_ref, cw_ref, cb_ref, wdn_ref, o_ref, carry, *, tm, fc, conv_k):
    F = wdn_ref.shape[0]
    D = wdn_ref.shape[1]
    n_chunks = F // fc
    R = tm // PHASES
    first_tail = PHASES - (conv_k - 1)

    @pl.when(pl.program_id(1) == 0)
    def _():
        carry[...] = jnp.zeros(carry.shape, F32)

    h2 = jnp.concatenate(_phase_slabs(h_ref, D), axis=0)
    hn = _rms(h2, g_ref[4:5, :]).astype(BF16)
    sub = lax.broadcasted_iota(jnp.int32, (1, SUBLANES, fc), 1)

    def up(j):
        return [jnp.dot(hn, wup_ref[:, lo:lo + fc], preferred_element_type=F32) for lo in (j * fc, F + j * fc)]

    def conv(u, lo):
        slabs = [u[v * R:(v + 1) * R, :] for v in range(PHASES)]
        tails = {w: carry[w - first_tail, :, lo:lo + fc] for w in range(first_tail, PHASES)}
        for w in range(first_tail, PHASES):
            carry[w - first_tail, :, lo:lo + fc] = slabs[w][R - SUBLANES:, :]
        taps = [cw_ref[k:k + 1, lo:lo + fc] for k in range(conv_k)]
        return jnp.concatenate(_phase_conv(slabs, tails, taps, sub, bias=cb_ref[:, lo:lo + fc]), axis=0)

    acc = None
    nxt = up(0)
    for j in range(n_chunks):
        ua, ub = nxt
        if j + 1 < n_chunks:
            nxt = up(j + 1)
        act = _silu(conv(ua, j * fc)) * conv(ub, F + j * fc)
        d = jnp.dot(act.astype(BF16), wdn_ref[j * fc:(j + 1) * fc, :], preferred_element_type=F32)
        acc = d if acc is None else acc + d
    out = h2 + _rms(acc, g_ref[5:6, :])
    for v in range(PHASES):
        o_ref[0, :, v * D:(v + 1) * D] = out[v * R:(v + 1) * R, :]


def _ffn(h, norm_g, w_up, conv_w, conv_b, w_down, *, tm, fc):
    B, T, D = h.shape
    F = w_down.shape[0]
    conv_k = conv_w.shape[0]
    assert conv_k - 1 < PHASES and tm % (PHASES * SUBLANES) == 0
    row_spec = pl.BlockSpec((1, tm // PHASES, PHASES * D), lambda b, i: (b, i, 0))
    kernel = functools.partial(_ffn_kernel, tm=tm, fc=fc, conv_k=conv_k)
    out = pl.pallas_call(
        kernel,
        grid=(B, T // tm),
        in_specs=[row_spec, _const_spec(norm_g.shape), _const_spec(w_up.shape), _const_spec(conv_w.shape),
                  _const_spec(conv_b.shape), _const_spec(w_down.shape)],
        out_specs=row_spec,
        out_shape=jax.ShapeDtypeStruct((B, T // PHASES, PHASES * D), F32),
        scratch_shapes=[pltpu.VMEM((conv_k - 1, SUBLANES, 2 * F), F32)],
        compiler_params=_params(2),
        name="ffn",
    )(h.reshape(B, T // PHASES, PHASES * D), norm_g, w_up, conv_w, conv_b, w_down)
    return out.reshape(B, T, D)


def _pick_tile(T, pref):
    t = min(T, pref)
    assert T % t == 0 and t % CHUNK == 0, (T, t)
    return t


def _layer(h, mem, g, w_in, gdn_conv_w, a_log, dt_bias, gdn_norm_g, cfm_dw_w, cfm_dw_b, cfm_ln_g, cfm_ln_b,
           w_out, mem_norm_g, xa_w_q, xa_w_kv, xa_w_o, ffn_w_up, ffn_conv_w, ffn_conv_b, ffn_w_down):
    B, T, D = h.shape
    s2 = 4 * GDN_WIDTH
    s4 = s2 + 2 * GDN_HEADS
    w_qkvg = w_in[:, :s2].astype(BF16)
    w_ab = w_in[:, s2:s4].T.astype(BF16)
    w_cfm = w_in[:, s4:].astype(BF16)
    pad = jnp.zeros((GDN_HEADS,), F32)
    a_log8 = jnp.concatenate([a_log.astype(F32), pad])[:, None]
    dt_bias8 = jnp.concatenate([dt_bias.astype(F32), pad])[:, None]

    tm = _pick_tile(T, 512)
    kv = _mem_kv(mem, mem_norm_g[None, :], xa_w_kv.astype(BF16))
    q, k, v, gate, gb, c = _mixer_in(h, g[0:1], w_qkvg, w_ab, w_cfm, gdn_conv_w, a_log8, dt_bias8,
                                     cfm_dw_w, cfm_dw_b[None, :], cfm_ln_g[None, :], cfm_ln_b[None, :], tm=tm)
    o_gdn = _gdn(q, k, v, gate, gb, gdn_norm_g[None, :], tg=_pick_tile(T, 2 * CHUNK))
    h = _attn(h, o_gdn.reshape(B, T, GDN_WIDTH), c, g, w_out.astype(BF16), kv, xa_w_q.astype(BF16),
              xa_w_o.astype(BF16), tm=tm)
    return _ffn(h, g, ffn_w_up.astype(BF16), ffn_conv_w, ffn_conv_b[None, :], ffn_w_down.astype(BF16),
                tm=tm, fc=256)


def kernel(x, mem, norm_g, w_in, gdn_conv_w, gdn_a_log, gdn_dt_bias, gdn_norm_g, cfm_dw_w, cfm_dw_b,
           cfm_ln_g, cfm_ln_b, w_out, mem_norm_g, xa_w_q, xa_w_kv, xa_w_o, ffn_w_up, ffn_conv_w,
           ffn_conv_b, ffn_w_down):
    h = x
    for l in range(norm_g.shape[0]):
        h = _layer(h, mem, norm_g[l], w_in[l], gdn_conv_w[l], gdn_a_log[l], gdn_dt_bias[l], gdn_norm_g[l],
                   cfm_dw_w[l], cfm_dw_b[l], cfm_ln_g[l], cfm_ln_b[l], w_out[l], mem_norm_g[l],
                   xa_w_q[l], xa_w_kv[l], xa_w_o[l], ffn_w_up[l], ffn_conv_w[l], ffn_conv_b[l], ffn_w_down[l])
    return h
```

```python
import functools

import jax
import jax.numpy as jnp
from jax import lax
from jax.experimental import pallas as pl
from jax.experimental.pallas import tpu as pltpu

EPS = 1e-6
GDN_HEADS = 4
GDN_HEAD_DIM = 128
GDN_WIDTH = GDN_HEADS * GDN_HEAD_DIM
GDN_SHORT_CONV = 4
CHUNK = 64
XATTN_HEADS = 4

F32 = jnp.float32
BF16 = jnp.bfloat16

SUBLANES = 8
PHASES = SUBLANES
VMEM_LIMIT_BYTES = 56 * 1024 * 1024

NT_DIMS = (((1,), (1,)), ((), ()))
TN_DIMS = (((0,), (0,)), ((), ()))


def _mm(a, b):
    return jnp.dot(a.astype(BF16), b.astype(BF16), preferred_element_type=F32)


def _mm_nt(a, b):
    return lax.dot_general(a.astype(BF16), b.astype(BF16), NT_DIMS, preferred_element_type=F32)


def _mm_tn(a, b):
    return lax.dot_general(a.astype(BF16), b.astype(BF16), TN_DIMS, preferred_element_type=F32)


def _rms(x, g):
    return x * lax.rsqrt(jnp.mean(x * x, axis=-1, keepdims=True) + EPS) * g


NEG_LOG2_E = -1.4426950408889634


def _sigmoid(x):
    return 1.0 / (1.0 + jnp.exp2(x * NEG_LOG2_E))


def _silu(x):
    return x * _sigmoid(x)


def _delay_rows(x, tail, s, sub):
    rows, width = x.shape
    groups = jnp.concatenate([tail, x], axis=0).reshape(rows // SUBLANES + 1, SUBLANES, width)
    rot = pltpu.roll(groups, s, 1)
    return jnp.where(sub < s, rot[:-1], rot[1:]).reshape(rows, width)


def _phase_slabs(ref, width):
    return [ref[0, :, v * width:(v + 1) * width] for v in range(PHASES)]


def _phase_conv(slabs, tails, taps, sub, bias=None):
    K = len(taps)
    delayed = {w: _delay_rows(slabs[w], tails[w], 1, sub) for w in range(PHASES - (K - 1), PHASES)}
    outs = []
    for v in range(PHASES):
        acc = taps[K - 1] * slabs[v]
        if bias is not None:
            acc = acc + bias
        for d in range(1, K):
            acc = acc + taps[K - 1 - d] * (slabs[v - d] if v >= d else delayed[v - d + PHASES])
        outs.append(acc)
    return outs


def _const_spec(shape):
    zeros = (0,) * len(shape)
    return pl.BlockSpec(shape, lambda *_: zeros)


def _params(n_axes):
    return pltpu.CompilerParams(dimension_semantics=("arbitrary",) * n_axes,
                                vmem_limit_bytes=VMEM_LIMIT_BYTES)


def _mem_kv_kernel(mem_ref, g_ref, w_ref, kv_ref):
    mn = _rms(mem_ref[0], g_ref[...])
    kv_ref[0] = _mm(mn, w_ref[...]).astype(BF16)


def _mem_kv(mem, g, w_kv):
    B, M, D = mem.shape
    N = w_kv.shape[1]
    return pl.pallas_call(
        _mem_kv_kernel,
        grid=(B,),
        in_specs=[pl.BlockSpec((1, M, D), lambda b: (b, 0, 0)), _const_spec((1, D)), _const_spec((D, N))],
        out_specs=pl.BlockSpec((1, M, N), lambda b: (b, 0, 0)),
        out_shape=jax.ShapeDtypeStruct((B, M, N), BF16),
        compiler_params=_params(1),
        name="mem_kv",
    )(mem, g, w_kv)


def _mixer_in_kernel(x_ref, g_ref, wqkvg_ref, wab_ref, wcfm_ref, convw_ref, alog_ref, dtb_ref,
                     dww_ref, dwb_ref, lng_ref, lnb_ref,
                     q_ref, k_ref, v_ref, gate_ref, gb_ref, c_ref, xf_ref,
                     ztail, ctail, cdel, perm, *, tm, cfm_k):
    qkv_w = 3 * GDN_WIDTH
    W = GDN_WIDTH
    D = x_ref.shape[2]
    cw = c_ref.shape[2] // PHASES
    R = tm // PHASES
    kc = GDN_SHORT_CONV
    first_tail = PHASES - (kc - 1)

    @pl.when(pl.program_id(1) == 0)
    def _():
        ztail[...] = jnp.zeros(ztail.shape, F32)
        ctail[...] = jnp.zeros(ctail.shape, F32)

    @pl.when((pl.program_id(0) == 0) & (pl.program_id(1) == 0))
    def _():
        src = lax.broadcasted_iota(jnp.int32, (tm, tm), 0)
        dst = lax.broadcasted_iota(jnp.int32, (tm, tm), 1)
        g_, v_, s_ = dst // CHUNK, (dst % CHUNK) // SUBLANES, dst % SUBLANES
        perm[...] = jnp.where(src == v_ * R + g_ * SUBLANES + s_, 1.0, 0.0).astype(F32)

    h = pltpu.einshape("jvc->vjc", x_ref[0].reshape(R, PHASES, D)).reshape(tm, D)
    for v in range(PHASES):
        xf_ref[0, :, v * D:(v + 1) * D] = h[v * R:(v + 1) * R, :]
    hb = _rms(h, g_ref[...]).astype(BF16)

    z = jnp.dot(hb, wqkvg_ref[...], preferred_element_type=F32)
    ab = lax.dot_general(wab_ref[...], hb, NT_DIMS, preferred_element_type=F32)
    cz = jnp.dot(hb, wcfm_ref[...], preferred_element_type=F32)

    sub_z = lax.broadcasted_iota(jnp.int32, (1, SUBLANES, qkv_w), 1)
    zs = [z[v * R:(v + 1) * R, :qkv_w] for v in range(PHASES)]
    tails = {w: ztail[w - first_tail] for w in range(first_tail, PHASES)}
    for w in range(first_tail, PHASES):
        ztail[w - first_tail] = zs[w][R - SUBLANES:, :]
    taps = [convw_ref[j:j + 1, :] for j in range(kc)]
    q_scale = GDN_HEAD_DIM ** -0.5
    for v, acc in enumerate(_phase_conv(zs, tails, taps, sub_z)):
        qkv = _silu(acc)
        gate_ref[0, :, v * W:(v + 1) * W] = _silu(z[v * R:(v + 1) * R, qkv_w:])
        for hd in range(GDN_HEADS):
            lo = hd * GDN_HEAD_DIM
            qh = qkv[:, lo:lo + GDN_HEAD_DIM]
            kh = qkv[:, W + lo:W + lo + GDN_HEAD_DIM]
            q_ref[0, :, v * W + lo:v * W + lo + GDN_HEAD_DIM] = qh * (
                lax.rsqrt(jnp.sum(qh * qh, axis=-1, keepdims=True) + EPS) * q_scale)
            k_ref[0, :, v * W + lo:v * W + lo + GDN_HEAD_DIM] = kh * lax.rsqrt(
                jnp.sum(kh * kh, axis=-1, keepdims=True) + EPS)
        v_ref[0, :, v * W:(v + 1) * W] = qkv[:, 2 * W:]

    ab = jnp.dot(ab, perm[...], precision=lax.Precision.HIGHEST, preferred_element_type=F32)
    sp_in = ab + dtb_ref[...]
    softplus = jnp.maximum(sp_in, 0.0) + jnp.log(1.0 + jnp.exp(-jnp.abs(sp_in)))
    g = -jnp.exp(alog_ref[...]) * softplus
    row = lax.broadcasted_iota(jnp.int32, ab.shape, 0)
    gb = jnp.where(row < GDN_HEADS, g, _sigmoid(ab))
    for c in range(tm // CHUNK):
        gb_ref[0, c] = gb[:, c * CHUNK:(c + 1) * CHUNK]

    sub_c = lax.broadcasted_iota(jnp.int32, (1, SUBLANES, cw), 1)
    n_del = cdel.shape[0]
    for w in range(PHASES):
        glu = cz[w * R:(w + 1) * R, :cw] * _sigmoid(cz[w * R:(w + 1) * R, cw:])
        tail = ctail[w]
        ctail[w] = glu[R - SUBLANES:, :]
        cdel[0, w * R:(w + 1) * R, :] = glu
        for m in range(1, n_del):
            cdel[m, w * R:(w + 1) * R, :] = _delay_rows(glu, tail, m, sub_c)
    convs = []
    for v in range(PHASES):
        acc = dwb_ref[...]
        for d in range(cfm_k):
            a, e = divmod(d, PHASES)
            w, m = (v - e) % PHASES, a + (1 if v < e else 0)
            acc = acc + dww_ref[cfm_k - 1 - d:cfm_k - d, :] * cdel[m, w * R:(w + 1) * R, :]
        convs.append(acc)
    conv = jnp.concatenate(convs, axis=0)
    mu = jnp.mean(conv, axis=-1, keepdims=True)
    xc = conv - mu
    var = jnp.mean(xc * xc, axis=-1, keepdims=True)
    cout = _silu(xc * lax.rsqrt(var + EPS) * lng_ref[...] + lnb_ref[...])
    for v in range(PHASES):
        c_ref[0, :, v * cw:(v + 1) * cw] = cout[v * R:(v + 1) * R, :]


def _mixer_in(x, g0, w_qkvg, w_ab, w_cfm, conv_w, a_log8, dt_bias8, dw_w, dw_b, ln_g, ln_b, *, tm):
    B, T, D = x.shape
    cfm_k, cw = dw_w.shape
    assert GDN_SHORT_CONV - 1 < PHASES and tm % (PHASES * SUBLANES) == 0
    n_del = (cfm_k - 1) // PHASES + 2
    assert n_del - 1 < SUBLANES
    R = tm // PHASES
    row_spec = lambda w: pl.BlockSpec((1, R, PHASES * w), lambda b, i: (b, i, 0))
    kernel = functools.partial(_mixer_in_kernel, tm=tm, cfm_k=cfm_k)
    wide = jax.ShapeDtypeStruct((B, T // PHASES, PHASES * GDN_WIDTH), F32)
    return pl.pallas_call(
        kernel,
        grid=(B, T // tm),
        in_specs=[pl.BlockSpec((1, tm, D), lambda b, i: (b, i, 0)), _const_spec(g0.shape), _const_spec(w_qkvg.shape), _const_spec(w_ab.shape),
                  _const_spec(w_cfm.shape), _const_spec(conv_w.shape), _const_spec(a_log8.shape),
                  _const_spec(dt_bias8.shape), _const_spec(dw_w.shape), _const_spec(dw_b.shape),
                  _const_spec(ln_g.shape), _const_spec(ln_b.shape)],
        out_specs=[row_spec(GDN_WIDTH), row_spec(GDN_WIDTH), row_spec(GDN_WIDTH), row_spec(GDN_WIDTH),
                   pl.BlockSpec((1, tm // CHUNK, 2 * GDN_HEADS, CHUNK), lambda b, i: (b, i, 0, 0)),
                   row_spec(cw), row_spec(D)],
        out_shape=[wide, wide, wide, wide,
                   jax.ShapeDtypeStruct((B, T // CHUNK, 2 * GDN_HEADS, CHUNK), F32),
                   jax.ShapeDtypeStruct((B, T // PHASES, PHASES * cw), F32),
                   jax.ShapeDtypeStruct((B, T // PHASES, PHASES * D), F32)],
        scratch_shapes=[pltpu.VMEM((GDN_SHORT_CONV - 1, SUBLANES, 3 * GDN_WIDTH), F32),
                        pltpu.VMEM((PHASES, SUBLANES, cw), F32),
                        pltpu.VMEM((n_del, tm, cw), F32),
                        pltpu.VMEM((tm, tm), F32)],
        compiler_params=_params(2),
        name="mixer_in",
    )(x, g0, w_qkvg, w_ab, w_cfm, conv_w, a_log8, dt_bias8, dw_w, dw_b, ln_g, ln_b)


def _unit_lower_inverse(Ls, eye, diag_mask, off_masks):
    Ps = [jnp.where(diag_mask, L, 0.0) for L in Ls]
    Ts = [eye - P for P in Ps]
    for _ in range(3):
        Ps = [_mm(P, P) for P in Ps]
        Ts = [T + _mm(T, P) for T, P in zip(Ts, Ps)]
    for m in off_masks:
        Xs = [_mm(T, jnp.where(m, L, 0.0)) for T, L in zip(Ts, Ls)]
        Ts = [T - _mm(X, T) for T, X in zip(Ts, Xs)]
    return Ts


def _gdn_kernel(q_ref, k_ref, v_ref, gate_ref, gb_ref, ng_ref, o_ref, s_ref, *, n_chunks):
    @pl.when(pl.program_id(0) == 0)
    def _():
        s_ref[...] = jnp.zeros(s_ref.shape, F32)

    C, H, Dh = CHUNK, GDN_HEADS, GDN_HEAD_DIM
    W = H * Dh
    rows_per_chunk = C // PHASES
    nb = q_ref.shape[0]
    units = [(b, h) for b in range(nb) for h in range(H)]
    ri = lax.broadcasted_iota(jnp.int32, (C, C), 0)
    ci = lax.broadcasted_iota(jnp.int32, (C, C), 1)
    ti = PHASES * (ri % rows_per_chunk) + ri // rows_per_chunk
    tj = PHASES * (ci % rows_per_chunk) + ci // rows_per_chunk
    causal = ti >= tj
    strict = ti > tj
    lower1 = jnp.where(causal, 1.0, 0.0).astype(F32)
    upper1 = jnp.where(ti <= tj, 1.0, 0.0).astype(F32)
    eye = jnp.where(ri == ci, 1.0, 0.0).astype(F32)
    diag_mask = (ti // 16) == (tj // 16)
    off_masks = [((ti // (2 * s)) == (tj // (2 * s))) & ((ti // s) > (tj // s)) for s in (16, 32)]
    hi = lax.Precision.HIGHEST
    ng = ng_ref[...]

    def load(ref, b, c, h):
        r0 = c * rows_per_chunk
        return jnp.concatenate([ref[b, r0:r0 + rows_per_chunk, p * W + h * Dh:p * W + (h + 1) * Dh]
                                for p in range(PHASES)], axis=0)

    for c in range(n_chunks):
        gc_row, gc_col, gb_col = [], [], []
        for b in range(nb):
            gbc = gb_ref[b, c]
            gc_row.append(jnp.dot(gbc, upper1, precision=hi, preferred_element_type=F32))
            gc_col.append(lax.dot_general(lower1, gbc, NT_DIMS, precision=hi, preferred_element_type=F32))
            gb_col.append(lax.dot_general(eye, gbc, NT_DIMS, precision=hi, preferred_element_type=F32))

        qgs, ks, kbqs, rhs, decays, kdecs, elast = [], [], [], [], [], [], []
        for b, h in units:
            qh = load(q_ref, b, c, h)
            kh = load(k_ref, b, c, h)
            vh = load(v_ref, b, c, h)
            gcol = gc_col[b][:, h:h + 1]
            grow = gc_row[b][h:h + 1, :]
            bcol = gb_col[b][:, H + h:H + h + 1]
            g_last = grow[:, C - 1:C]
            egc = jnp.exp(gcol)
            kb = kh * bcol
            decays.append(jnp.where(causal, jnp.exp(jnp.where(causal, gcol - grow, 0.0)), 0.0))
            qgs.append(qh * egc)
            ks.append(kh.astype(BF16))
            kbqs.append(jnp.concatenate([kb, qh], axis=0))
            rhs.append(jnp.concatenate([vh * bcol, kb * egc], axis=1))
            kdecs.append(kh * jnp.exp(g_last - gcol))
            elast.append(jnp.exp(g_last))
        prods = [_mm_nt(a, k) for a, k in zip(kbqs, ks)]
        Ls = [jnp.where(strict, p[:C] * d, 0.0) for p, d in zip(prods, decays)]
        attns = [p[C:] * d for p, d in zip(prods, decays)]
        Ts = _unit_lower_inverse(Ls, eye, diag_mask, off_masks)
        sols = [_mm(T, r) for T, r in zip(Ts, rhs)]

        Ss = [s_ref[i] for i in range(len(units))]
        ws_qs = [_mm(jnp.concatenate([sol[:, Dh:], qg], axis=0), S) for sol, qg, S in zip(sols, qgs, Ss)]
        v_new = [sol[:, :Dh] - x[:C] for sol, x in zip(sols, ws_qs)]
        v_new_b = [x.astype(BF16) for x in v_new]
        outs = [x[C:] + _mm(a, vb) for x, a, vb in zip(ws_qs, attns, v_new_b)]
        for i, (S, e, kd, vb) in enumerate(zip(Ss, elast, kdecs, v_new_b)):
            s_ref[i] = S * e + _mm_tn(kd, vb)
        r0 = c * rows_per_chunk
        for (b, h), o in zip(units, outs):
            on = _rms(o, ng) * load(gate_ref, b, c, h)
            for p in range(PHASES):
                o_ref[b, r0:r0 + rows_per_chunk, p * W + h * Dh:p * W + (h + 1) * Dh] = (
                    on[p * rows_per_chunk:(p + 1) * rows_per_chunk])


def _gdn(q, k, v, gate, gb, norm_g, *, tg):
    B, TP, WP = q.shape
    n_chunks = tg // CHUNK
    row_spec = pl.BlockSpec((B, tg // PHASES, WP), lambda i: (0, i, 0))
    return pl.pallas_call(
        functools.partial(_gdn_kernel, n_chunks=n_chunks),
        grid=(TP * PHASES // tg,),
        in_specs=[row_spec, row_spec, row_spec, row_spec,
                  pl.BlockSpec((B, n_chunks, 2 * GDN_HEADS, CHUNK), lambda i: (0, i, 0, 0)),
                  _const_spec(norm_g.shape)],
        out_specs=row_spec,
        out_shape=jax.ShapeDtypeStruct((B, TP, WP), F32),
        scratch_shapes=[pltpu.VMEM((B * GDN_HEADS, GDN_HEAD_DIM, GDN_HEAD_DIM), F32)],
        compiler_params=_params(1),
        name="gdn",
    )(q, k, v, gate, gb, norm_g)


def _attn_kernel(x_ref, og_ref, c_ref, g_ref, wout_ref, kv_ref, wq_ref, wo_ref, h_ref, *, n_split):
    D = x_ref.shape[2] // PHASES
    R = x_ref.shape[1]
    dh = D // XATTN_HEADS
    per = PHASES // n_split
    groups = [range(i * per, (i + 1) * per) for i in range(n_split)]
    hs = range(XATTN_HEADS)

    def rows(ref, grp):
        w = ref.shape[2] // PHASES
        return jnp.concatenate([ref[0, :, v * w:(v + 1) * w] for v in grp], axis=0)

    mixes = [jnp.concatenate([rows(og_ref, grp).astype(BF16), rows(c_ref, grp).astype(BF16)], axis=1)
             for grp in groups]
    ys = [jnp.dot(m, wout_ref[...], preferred_element_type=F32) for m in mixes]
    h1s = [rows(x_ref, grp) + _rms(y, g_ref[1:2, :]) for grp, y in zip(groups, ys)]
    hns = [_rms(h1, g_ref[2:3, :]).astype(BF16) for h1 in h1s]
    qs = [(jnp.dot(hn, wq_ref[...], preferred_element_type=F32) * (dh ** -0.5)).astype(BF16) for hn in hns]
    ss = [[_mm_nt(q[:, hd * dh:(hd + 1) * dh], kv_ref[0, :, hd * dh:(hd + 1) * dh]) for hd in hs] for q in qs]
    es = [[jnp.exp(s - jnp.max(s, axis=-1, keepdims=True)) for s in sg] for sg in ss]
    pvs = [[_mm(e, kv_ref[0, :, D + hd * dh:D + (hd + 1) * dh]) for e, hd in zip(eg, hs)] for eg in es]
    os_ = [jnp.concatenate([o * (1.0 / jnp.sum(e, axis=-1, keepdims=True)) for o, e in zip(pg, eg)], axis=1)
           for pg, eg in zip(pvs, es)]
    y2s = [_mm(o, wo_ref[...]) for o in os_]
    for grp, h1, y2 in zip(groups, h1s, y2s):
        out = h1 + _rms(y2, g_ref[3:4, :])
        for i, v in enumerate(grp):
            h_ref[0, :, v * D:(v + 1) * D] = out[i * R:(i + 1) * R, :]


def _attn(x, o_gdn, c, norm_g, w_out, kv, w_q, w_o, *, tm):
    B, TP, DP = x.shape
    M = kv.shape[1]
    row_spec = lambda a: pl.BlockSpec((1, tm // PHASES, a.shape[2]), lambda b, i: (b, i, 0))
    return pl.pallas_call(
        functools.partial(_attn_kernel, n_split=2),
        grid=(B, TP * PHASES // tm),
        in_specs=[row_spec(x), row_spec(o_gdn), row_spec(c), _const_spec(norm_g.shape),
                  _const_spec(w_out.shape), pl.BlockSpec((1, M, kv.shape[2]), lambda b, i: (b, 0, 0)),
                  _const_spec(w_q.shape), _const_spec(w_o.shape)],
        out_specs=row_spec(x),
        out_shape=jax.ShapeDtypeStruct((B, TP, DP), F32),
        compiler_params=_params(2),
        name="attn",
    )(x, o_gdn, c, norm_g, w_out, kv, w_q, w_o)


def _ffn_kernel(h_ref, g_ref, wup_ref, cw_ref, cb_ref, wdn_ref, o_ref, carry, *, tm, fc, conv_k):
    F = wdn_ref.shape[0]
    D = wdn_ref.shape[1]
    n_chunks = F // fc
    R = tm // PHASES
    first_tail = PHASES - (conv_k - 1)

    @pl.when(pl.program_id(1) == 0)
    def _():
        carry[...] = jnp.zeros(carry.shape, F32)

    h2 = jnp.concatenate(_phase_slabs(h_ref, D), axis=0)
    hn = _rms(h2, g_ref[4:5, :]).astype(BF16)
    sub = lax.broadcasted_iota(jnp.int32, (1, SUBLANES, fc), 1)

    def up(j):
        return [jnp.dot(hn, wup_ref[:, lo:lo + fc], preferred_element_type=F32) for lo in (j * fc, F + j * fc)]

    def conv(u, lo):
        slabs = [u[v * R:(v + 1) * R, :] for v in range(PHASES)]
        tails = {w: carry[w - first_tail, :, lo:lo + fc] for w in range(first_tail, PHASES)}
        for w in range(first_tail, PHASES):
            carry[w - first_tail, :, lo:lo + fc] = slabs[w][R - SUBLANES:, :]
        taps = [cw_ref[k:k + 1, lo:lo + fc] for k in range(conv_k)]
        return jnp.concatenate(_phase_conv(slabs, tails, taps, sub, bias=cb_ref[:, lo:lo + fc]), axis=0)

    acc = None
    nxt = up(0)
    for j in range(n_chunks):
        ua, ub = nxt
        if j + 1 < n_chunks:
            nxt = up(j + 1)
        act = _silu(conv(ua, j * fc)) * conv(ub, F + j * fc)
        d = jnp.dot(act.astype(BF16), wdn_ref[j * fc:(j + 1) * fc, :], preferred_element_type=F32)
        acc = d if acc is None else acc + d
    out = h2 + _rms(acc, g_ref[5:6, :])
    o_ref[0] = pltpu.einshape("vjc->jvc", out.reshape(PHASES, R, D)).reshape(tm, D)


def _ffn(h, norm_g, w_up, conv_w, conv_b, w_down, *, tm, fc):
    B, TP, DP = h.shape
    T, D = TP * PHASES, DP // PHASES
    F = w_down.shape[0]
    conv_k = conv_w.shape[0]
    assert conv_k - 1 < PHASES and tm % (PHASES * SUBLANES) == 0
    row_spec = pl.BlockSpec((1, tm // PHASES, PHASES * D), lambda b, i: (b, i, 0))
    kernel = functools.partial(_ffn_kernel, tm=tm, fc=fc, conv_k=conv_k)
    return pl.pallas_call(
        kernel,
        grid=(B, T // tm),
        in_specs=[row_spec, _const_spec(norm_g.shape), _const_spec(w_up.shape), _const_spec(conv_w.shape),
                  _const_spec(conv_b.shape), _const_spec(w_down.shape)],
        out_specs=pl.BlockSpec((1, tm, D), lambda b, i: (b, i, 0)),
        out_shape=jax.ShapeDtypeStruct((B, T, D), F32),
        scratch_shapes=[pltpu.VMEM((conv_k - 1, SUBLANES, 2 * F), F32)],
        compiler_params=_params(2),
        name="ffn",
    )(h, norm_g, w_up, conv_w, conv_b, w_down)


def _pick_tile(T, pref):
    t = min(T, pref)
    assert T % t == 0 and t % CHUNK == 0, (T, t)
    return t


def _layer(h, mem, g, w_in, gdn_conv_w, a_log, dt_bias, gdn_norm_g, cfm_dw_w, cfm_dw_b, cfm_ln_g, cfm_ln_b,
           w_out, mem_norm_g, xa_w_q, xa_w_kv, xa_w_o, ffn_w_up, ffn_conv_w, ffn_conv_b, ffn_w_down):
    B, T, D = h.shape
    s2 = 4 * GDN_WIDTH
    s4 = s2 + 2 * GDN_HEADS
    w_qkvg = w_in[:, :s2].astype(BF16)
    w_ab = w_in[:, s2:s4].T.astype(BF16)
    w_cfm = w_in[:, s4:].astype(BF16)
    pad = jnp.zeros((GDN_HEADS,), F32)
    a_log8 = jnp.concatenate([a_log.astype(F32), pad])[:, None]
    dt_bias8 = jnp.concatenate([dt_bias.astype(F32), pad])[:, None]

    tm = _pick_tile(T, 512)
    kv = _mem_kv(mem, mem_norm_g[None, :], xa_w_kv.astype(BF16))
    q, k, v, gate, gb, c, hf = _mixer_in(h, g[0:1], w_qkvg, w_ab, w_cfm, gdn_conv_w, a_log8, dt_bias8,
                                         cfm_dw_w, cfm_dw_b[None, :], cfm_ln_g[None, :], cfm_ln_b[None, :], tm=tm)
    o_gdn = _gdn(q, k, v, gate, gb, gdn_norm_g[None, :], tg=_pick_tile(T, 2 * CHUNK))
    hf = _attn(hf, o_gdn, c, g, w_out.astype(BF16), kv, xa_w_q.astype(BF16), xa_w_o.astype(BF16), tm=tm)
    return _ffn(hf, g, ffn_w_up.astype(BF16), ffn_conv_w, ffn_conv_b[None, :], ffn_w_down.astype(BF16),
                tm=tm, fc=256)


def kernel(x, mem, norm_g, w_in, gdn_conv_w, gdn_a_log, gdn_dt_bias, gdn_norm_g, cfm_dw_w, cfm_dw_b,
           cfm_ln_g, cfm_ln_b, w_out, mem_norm_g, xa_w_q, xa_w_kv, xa_w_o, ffn_w_up, ffn_conv_w,
           ffn_conv_b, ffn_w_down):
    h = x
    for l in range(norm_g.shape[0]):
        h = _layer(h, mem, norm_g[l], w_in[l], gdn_conv_w[l], gdn_a_log[l], gdn_dt_bias[l], gdn_norm_g[l],
                   cfm_dw_w[l], cfm_dw_b[l], cfm_ln_g[l], cfm_ln_b[l], w_out[l], mem_norm_g[l],
                   xa_w_q[l], xa_w_kv[l], xa_w_o[l], ffn_w_up[l], ffn_conv_w[l], ffn_conv_b[l], ffn_w_down[l])
    return h
```

```python
import functools

import jax
import jax.numpy as jnp
from jax import lax
from jax.experimental import pallas as pl
from jax.experimental.pallas import tpu as pltpu

EPS = 1e-6
GDN_HEADS = 4
GDN_HEAD_DIM = 128
GDN_WIDTH = GDN_HEADS * GDN_HEAD_DIM
GDN_SHORT_CONV = 4
CHUNK = 64
XATTN_HEADS = 4

F32 = jnp.float32
BF16 = jnp.bfloat16

SUBLANES = 8
PHASES = SUBLANES
VMEM_LIMIT_BYTES = 56 * 1024 * 1024

NT_DIMS = (((1,), (1,)), ((), ()))
TN_DIMS = (((0,), (0,)), ((), ()))


def _mm(a, b):
    return jnp.dot(a.astype(BF16), b.astype(BF16), preferred_element_type=F32)


def _mm_nt(a, b):
    return lax.dot_general(a.astype(BF16), b.astype(BF16), NT_DIMS, preferred_element_type=F32)


def _mm_tn(a, b):
    return lax.dot_general(a.astype(BF16), b.astype(BF16), TN_DIMS, preferred_element_type=F32)


def _rms(x, g):
    return x * lax.rsqrt(jnp.mean(x * x, axis=-1, keepdims=True) + EPS) * g


NEG_LOG2_E = -1.4426950408889634


def _sigmoid(x):
    return 1.0 / (1.0 + jnp.exp2(x * NEG_LOG2_E))


def _silu(x):
    return x * _sigmoid(x)


def _delay_rows(x, tail, s, sub):
    rows, width = x.shape
    groups = jnp.concatenate([tail, x], axis=0).reshape(rows // SUBLANES + 1, SUBLANES, width)
    rot = pltpu.roll(groups, s, 1)
    return jnp.where(sub < s, rot[:-1], rot[1:]).reshape(rows, width)


def _phase_slabs(ref, width):
    return [ref[0, :, v * width:(v + 1) * width] for v in range(PHASES)]


def _phase_conv(slabs, tails, taps, sub, bias=None):
    K = len(taps)
    delayed = {w: _delay_rows(slabs[w], tails[w], 1, sub) for w in range(PHASES - (K - 1), PHASES)}
    outs = []
    for v in range(PHASES):
        acc = taps[K - 1] * slabs[v]
        if bias is not None:
            acc = acc + bias
        for d in range(1, K):
            acc = acc + taps[K - 1 - d] * (slabs[v - d] if v >= d else delayed[v - d + PHASES])
        outs.append(acc)
    return outs


def _const_spec(shape):
    zeros = (0,) * len(shape)
    return pl.BlockSpec(shape, lambda *_: zeros)


def _params(n_axes):
    return pltpu.CompilerParams(dimension_semantics=("arbitrary",) * n_axes,
                                vmem_limit_bytes=VMEM_LIMIT_BYTES)


def _mem_kv_kernel(mem_ref, g_ref, w_ref, kv_ref):
    mn = _rms(mem_ref[0], g_ref[...])
    kv_ref[0] = _mm(mn, w_ref[...]).astype(BF16)


def _mem_kv(mem, g, w_kv):
    B, M, D = mem.shape
    N = w_kv.shape[1]
    return pl.pallas_call(
        _mem_kv_kernel,
        grid=(B,),
        in_specs=[pl.BlockSpec((1, M, D), lambda b: (b, 0, 0)), _const_spec((1, D)), _const_spec((D, N))],
        out_specs=pl.BlockSpec((1, M, N), lambda b: (b, 0, 0)),
        out_shape=jax.ShapeDtypeStruct((B, M, N), BF16),
        compiler_params=_params(1),
        name="mem_kv",
    )(mem, g, w_kv)


def _mixer_in_kernel(x_ref, g_ref, wqkvg_ref, wab_ref, wcfm_ref, convw_ref, alog_ref, dtb_ref,
                     dww_ref, dwb_ref, lng_ref, lnb_ref,
                     q_ref, k_ref, v_ref, gate_ref, gb_ref, c_ref, xf_ref,
                     ztail, ctail, cdel, perm, *, tm, cfm_k):
    qkv_w = 3 * GDN_WIDTH
    W = GDN_WIDTH
    D = x_ref.shape[2]
    cw = c_ref.shape[2] // PHASES
    R = tm // PHASES
    kc = GDN_SHORT_CONV
    first_tail = PHASES - (kc - 1)

    @pl.when(pl.program_id(1) == 0)
    def _():
        ztail[...] = jnp.zeros(ztail.shape, F32)
        ctail[...] = jnp.zeros(ctail.shape, F32)

    @pl.when((pl.program_id(0) == 0) & (pl.program_id(1) == 0))
    def _():
        src = lax.broadcasted_iota(jnp.int32, (tm, tm), 0)
        dst = lax.broadcasted_iota(jnp.int32, (tm, tm), 1)
        g_, v_, s_ = dst // CHUNK, (dst % CHUNK) // SUBLANES, dst % SUBLANES
        perm[...] = jnp.where(src == v_ * R + g_ * SUBLANES + s_, 1.0, 0.0).astype(F32)

    h = pltpu.einshape("jvc->vjc", x_ref[0].reshape(R, PHASES, D)).reshape(tm, D)
    for v in range(PHASES):
        xf_ref[0, :, v * D:(v + 1) * D] = h[v * R:(v + 1) * R, :]
    hb = _rms(h, g_ref[...]).astype(BF16)

    z = jnp.dot(hb, wqkvg_ref[...], preferred_element_type=F32)
    ab = lax.dot_general(wab_ref[...], hb, NT_DIMS, preferred_element_type=F32)
    cz = jnp.dot(hb, wcfm_ref[...], preferred_element_type=F32)

    sub_z = lax.broadcasted_iota(jnp.int32, (1, SUBLANES, qkv_w), 1)
    zs = [z[v * R:(v + 1) * R, :qkv_w] for v in range(PHASES)]
    tails = {w: ztail[w - first_tail] for w in range(first_tail, PHASES)}
    for w in range(first_tail, PHASES):
        ztail[w - first_tail] = zs[w][R - SUBLANES:, :]
    taps = [convw_ref[j:j + 1, :] for j in range(kc)]
    q_scale = GDN_HEAD_DIM ** -0.5
    for v, acc in enumerate(_phase_conv(zs, tails, taps, sub_z)):
        qkv = _silu(acc)
        gate_ref[0, :, v * W:(v + 1) * W] = _silu(z[v * R:(v + 1) * R, qkv_w:])
        for hd in range(GDN_HEADS):
            lo = hd * GDN_HEAD_DIM
            qh = qkv[:, lo:lo + GDN_HEAD_DIM]
            kh = qkv[:, W + lo:W + lo + GDN_HEAD_DIM]
            q_ref[0, :, v * W + lo:v * W + lo + GDN_HEAD_DIM] = qh * (
                lax.rsqrt(jnp.sum(qh * qh, axis=-1, keepdims=True) + EPS) * q_scale)
            k_ref[0, :, v * W + lo:v * W + lo + GDN_HEAD_DIM] = kh * lax.rsqrt(
                jnp.sum(kh * kh, axis=-1, keepdims=True) + EPS)
        v_ref[0, :, v * W:(v + 1) * W] = qkv[:, 2 * W:]

    ab = jnp.dot(ab, perm[...], precision=lax.Precision.HIGHEST, preferred_element_type=F32)
    sp_in = ab + dtb_ref[...]
    softplus = jnp.maximum(sp_in, 0.0) + jnp.log(1.0 + jnp.exp(-jnp.abs(sp_in)))
    g = -jnp.exp(alog_ref[...]) * softplus
    row = lax.broadcasted_iota(jnp.int32, ab.shape, 0)
    gb = jnp.where(row < GDN_HEADS, g, _sigmoid(ab))
    for c in range(tm // CHUNK):
        gb_ref[0, c] = gb[:, c * CHUNK:(c + 1) * CHUNK]

    sub_c = lax.broadcasted_iota(jnp.int32, (1, SUBLANES, cw), 1)
    n_del = cdel.shape[0]
    for w in range(PHASES):
        glu = cz[w * R:(w + 1) * R, :cw] * _sigmoid(cz[w * R:(w + 1) * R, cw:])
        tail = ctail[w]
        ctail[w] = glu[R - SUBLANES:, :]
        cdel[0, w * R:(w + 1) * R, :] = glu
        for m in range(1, n_del):
            cdel[m, w * R:(w + 1) * R, :] = _delay_rows(glu, tail, m, sub_c)
    convs = []
    for v in range(PHASES):
        acc = dwb_ref[...]
        for d in range(cfm_k):
            a, e = divmod(d, PHASES)
            w, m = (v - e) % PHASES, a + (1 if v < e else 0)
            acc = acc + dww_ref[cfm_k - 1 - d:cfm_k - d, :] * cdel[m, w * R:(w + 1) * R, :]
        convs.append(acc)
    conv = jnp.concatenate(convs, axis=0)
    mu = jnp.mean(conv, axis=-1, keepdims=True)
    xc = conv - mu
    var = jnp.mean(xc * xc, axis=-1, keepdims=True)
    cout = _silu(xc * lax.rsqrt(var + EPS) * lng_ref[...] + lnb_ref[...])
    for v in range(PHASES):
        c_ref[0, :, v * cw:(v + 1) * cw] = cout[v * R:(v + 1) * R, :]


def _mixer_in(x, g0, w_qkvg, w_ab, w_cfm, conv_w, a_log8, dt_bias8, dw_w, dw_b, ln_g, ln_b, *, tm):
    B, T, D = x.shape
    cfm_k, cw = dw_w.shape
    assert GDN_SHORT_CONV - 1 < PHASES and tm % (PHASES * SUBLANES) == 0
    n_del = (cfm_k - 1) // PHASES + 2
    assert n_del - 1 < SUBLANES
    R = tm // PHASES
    row_spec = lambda w: pl.BlockSpec((1, R, PHASES * w), lambda b, i: (b, i, 0))
    kernel = functools.partial(_mixer_in_kernel, tm=tm, cfm_k=cfm_k)
    wide = jax.ShapeDtypeStruct((B, T // PHASES, PHASES * GDN_WIDTH), F32)
    return pl.pallas_call(
        kernel,
        grid=(B, T // tm),
        in_specs=[pl.BlockSpec((1, tm, D), lambda b, i: (b, i, 0)), _const_spec(g0.shape), _const_spec(w_qkvg.shape), _const_spec(w_ab.shape),
                  _const_spec(w_cfm.shape), _const_spec(conv_w.shape), _const_spec(a_log8.shape),
                  _const_spec(dt_bias8.shape), _const_spec(dw_w.shape), _const_spec(dw_b.shape),
                  _const_spec(ln_g.shape), _const_spec(ln_b.shape)],
        out_specs=[row_spec(GDN_WIDTH), row_spec(GDN_WIDTH), row_spec(GDN_WIDTH), row_spec(GDN_WIDTH),
                   pl.BlockSpec((1, tm // CHUNK, 2 * GDN_HEADS, CHUNK), lambda b, i: (b, i, 0, 0)),
                   row_spec(cw), row_spec(D)],
        out_shape=[wide, wide, wide, wide,
                   jax.ShapeDtypeStruct((B, T // CHUNK, 2 * GDN_HEADS, CHUNK), F32),
                   jax.ShapeDtypeStruct((B, T // PHASES, PHASES * cw), F32),
                   jax.ShapeDtypeStruct((B, T // PHASES, PHASES * D), F32)],
        scratch_shapes=[pltpu.VMEM((GDN_SHORT_CONV - 1, SUBLANES, 3 * GDN_WIDTH), F32),
                        pltpu.VMEM((PHASES, SUBLANES, cw), F32),
                        pltpu.VMEM((n_del, tm, cw), F32),
                        pltpu.VMEM((tm, tm), F32)],
        compiler_params=_params(2),
        name="mixer_in",
    )(x, g0, w_qkvg, w_ab, w_cfm, conv_w, a_log8, dt_bias8, dw_w, dw_b, ln_g, ln_b)


def _block_diag(x4, lane_masks):
    xb = x4.astype(BF16)
    return jnp.concatenate([xb * m for m in lane_masks], axis=0)


def _packed_lower_inverse(Ls, eye, diag_mask, off_masks, lane_masks):
    mmp = lambda a, bd: jnp.dot(a.astype(BF16), bd, preferred_element_type=F32)
    Ps = [jnp.where(diag_mask, L, 0.0) for L in Ls]
    Ts = [eye - P for P in Ps]
    for _ in range(3):
        Ps = [mmp(P, _block_diag(P, lane_masks)) for P in Ps]
        Bs = [_block_diag(P, lane_masks) for P in Ps]
        Ts = [T + mmp(T, B) for T, B in zip(Ts, Bs)]
    for m in off_masks:
        Xs = [mmp(T, _block_diag(jnp.where(m, L, 0.0), lane_masks)) for T, L in zip(Ts, Ls)]
        Ts = [T - mmp(X, _block_diag(T, lane_masks)) for T, X in zip(Ts, Xs)]
    return Ts


def _gdn_kernel(q_ref, k_ref, v_ref, gate_ref, gb_ref, ng_ref, o_ref, s_ref, *, n_chunks):
    @pl.when(pl.program_id(0) == 0)
    def _():
        s_ref[...] = jnp.zeros(s_ref.shape, F32)

    C, H, Dh = CHUNK, GDN_HEADS, GDN_HEAD_DIM
    W = H * Dh
    rows_per_chunk = C // PHASES
    nb = q_ref.shape[0]
    ri = lax.broadcasted_iota(jnp.int32, (C, H * C), 0)
    li = lax.broadcasted_iota(jnp.int32, (C, H * C), 1)
    lb, ci = li // C, li % C
    ti = PHASES * (ri % rows_per_chunk) + ri // rows_per_chunk
    tj = PHASES * (ci % rows_per_chunk) + ci // rows_per_chunk
    causal = ti >= tj
    strict = ti > tj
    eye = jnp.where(ri == ci, 1.0, 0.0).astype(F32)
    diag_mask = (ti // 16) == (tj // 16)
    off_masks = [((ti // (2 * s)) == (tj // (2 * s))) & ((ti // s) > (tj // s)) for s in (16, 32)]
    lane_masks = [jnp.where(lb == u, 1.0, 0.0).astype(BF16) for u in range(H)]
    lower1 = jnp.where(causal[:, :C], 1.0, 0.0).astype(F32)
    upper4 = jnp.where(ti <= tj, 1.0, 0.0).astype(F32)
    eye1 = eye[:, :C]
    lb_row = lb[0:1, :]
    hi = lax.Precision.HIGHEST
    ng = ng_ref[...]
    zero_k = jnp.zeros((C, Dh), BF16)

    def load(ref, b, c, h):
        r0 = c * rows_per_chunk
        return jnp.concatenate([ref[b, r0:r0 + rows_per_chunk, p * W + h * Dh:p * W + (h + 1) * Dh]
                                for p in range(PHASES)], axis=0)

    def pick_head(parts):
        out = parts[H - 1]
        for u in range(H - 2, -1, -1):
            out = jnp.where(lb == u, parts[u], out)
        return out

    groups = [(c, b) for c in range(n_chunks) for b in range(nb)]
    A4s, KKs, decays, rhss, qgss, kdecss, elasts = [], [], [], [], [], [], []
    for c, b in groups:
        gbc = gb_ref[b, c]
        gc_rows = jnp.dot(gbc, upper4, precision=hi, preferred_element_type=F32)
        gc_col = lax.dot_general(lower1, gbc, NT_DIMS, precision=hi, preferred_element_type=F32)
        gb_col = lax.dot_general(eye1, gbc, NT_DIMS, precision=hi, preferred_element_type=F32)
        grow4 = gc_rows[H - 1:H, :]
        for u in range(H - 2, -1, -1):
            grow4 = jnp.where(lb_row == u, gc_rows[u:u + 1, :], grow4)
        gcol4 = pick_head([gc_col[:, u:u + 1] for u in range(H)])
        decays.append(jnp.where(causal, jnp.exp(jnp.where(causal, gcol4 - grow4, 0.0)), 0.0))
        kbqs, kks, rhs, qgs, kdecs, els = [], [], [], [], [], []
        for h in range(H):
            qh = load(q_ref, b, c, h)
            kh = load(k_ref, b, c, h)
            vh = load(v_ref, b, c, h)
            gcol = gc_col[:, h:h + 1]
            bcol = gb_col[:, H + h:H + h + 1]
            g_last = gc_rows[h:h + 1, C - 1:C]
            egc = jnp.exp(gcol)
            kb = kh * bcol
            kbqs.append(jnp.concatenate([kb, qh], axis=0).astype(BF16))
            kb16 = kh.astype(BF16)
            kks.append(jnp.concatenate([kb16 if u == h else zero_k for u in range(H)], axis=1))
            rhs.append(jnp.concatenate([vh * bcol, kb * egc], axis=1).astype(BF16))
            qgs.append(qh * egc)
            kdecs.append(kh * jnp.exp(g_last - gcol))
            els.append(jnp.exp(g_last))
        A4s.append(jnp.concatenate(kbqs, axis=1))
        KKs.append(jnp.concatenate(kks, axis=0))
        rhss.append(jnp.concatenate(rhs, axis=0))
        qgss.append(qgs)
        kdecss.append(kdecs)
        elasts.append(els)
    prods = [lax.dot_general(a, kk, NT_DIMS, preferred_element_type=F32) for a, kk in zip(A4s, KKs)]
    Ls = [jnp.where(strict, p[:C] * d, 0.0) for p, d in zip(prods, decays)]
    attns = [_block_diag(p[C:] * d, lane_masks) for p, d in zip(prods, decays)]
    Ts = _packed_lower_inverse(Ls, eye, diag_mask, off_masks, lane_masks)
    sols = [jnp.dot(_block_diag(T, lane_masks), r, preferred_element_type=F32) for T, r in zip(Ts, rhss)]

    for c in range(n_chunks):
        gi = [c * nb + b for b in range(nb)]
        Ss = [[s_ref[b * H + h] for h in range(H)] for b in range(nb)]
        ws_qs = [[_mm(jnp.concatenate([sols[g][h * C:(h + 1) * C, Dh:], qgss[g][h]], axis=0), Ss[b][h])
                  for h in range(H)] for b, g in enumerate(gi)]
        v_new = [[(sols[g][h * C:(h + 1) * C, :Dh] - ws_qs[b][h][:C]).astype(BF16) for h in range(H)]
                 for b, g in enumerate(gi)]
        o_att = [jnp.dot(attns[g], jnp.concatenate(v_new[b], axis=0), preferred_element_type=F32)
                 for b, g in enumerate(gi)]
        for b, g in enumerate(gi):
            for h in range(H):
                s_ref[b * H + h] = Ss[b][h] * elasts[g][h] + _mm_tn(kdecss[g][h], v_new[b][h])
        r0 = c * rows_per_chunk
        for b, g in enumerate(gi):
            for h in range(H):
                o = ws_qs[b][h][C:] + o_att[b][h * C:(h + 1) * C]
                on = _rms(o, ng) * load(gate_ref, b, c, h)
                for p in range(PHASES):
                    o_ref[b, r0:r0 + rows_per_chunk, p * W + h * Dh:p * W + (h + 1) * Dh] = (
                        on[p * rows_per_chunk:(p + 1) * rows_per_chunk])


def _gdn(q, k, v, gate, gb, norm_g, *, tg):
    B, TP, WP = q.shape
    n_chunks = tg // CHUNK
    row_spec = pl.BlockSpec((B, tg // PHASES, WP), lambda i: (0, i, 0))
    return pl.pallas_call(
        functools.partial(_gdn_kernel, n_chunks=n_chunks),
        grid=(TP * PHASES // tg,),
        in_specs=[row_spec, row_spec, row_spec, row_spec,
                  pl.BlockSpec((B, n_chunks, 2 * GDN_HEADS, CHUNK), lambda i: (0, i, 0, 0)),
                  _const_spec(norm_g.shape)],
        out_specs=row_spec,
        out_shape=jax.ShapeDtypeStruct((B, TP, WP), F32),
        scratch_shapes=[pltpu.VMEM((B * GDN_HEADS, GDN_HEAD_DIM, GDN_HEAD_DIM), F32)],
        compiler_params=_params(1),
        name="gdn",
    )(q, k, v, gate, gb, norm_g)


def _attn_kernel(x_ref, og_ref, c_ref, g_ref, wout_ref, kv_ref, wq_ref, wo_ref, h_ref, *, n_split):
    D = x_ref.shape[2] // PHASES
    R = x_ref.shape[1]
    dh = D // XATTN_HEADS
    per = PHASES // n_split
    groups = [range(i * per, (i + 1) * per) for i in range(n_split)]
    hs = range(XATTN_HEADS)

    def rows(ref, grp):
        w = ref.shape[2] // PHASES
        return jnp.concatenate([ref[0, :, v * w:(v + 1) * w] for v in grp], axis=0)

    mixes = [jnp.concatenate([rows(og_ref, grp).astype(BF16), rows(c_ref, grp).astype(BF16)], axis=1)
             for grp in groups]
    ys = [jnp.dot(m, wout_ref[...], preferred_element_type=F32) for m in mixes]
    h1s = [rows(x_ref, grp) + _rms(y, g_ref[1:2, :]) for grp, y in zip(groups, ys)]
    hns = [_rms(h1, g_ref[2:3, :]).astype(BF16) for h1 in h1s]
    qs = [(jnp.dot(hn, wq_ref[...], preferred_element_type=F32) * (dh ** -0.5)).astype(BF16) for hn in hns]
    ss = [[_mm_nt(q[:, hd * dh:(hd + 1) * dh], kv_ref[0, :, hd * dh:(hd + 1) * dh]) for hd in hs] for q in qs]
    es = [[jnp.exp(s - jnp.max(s, axis=-1, keepdims=True)) for s in sg] for sg in ss]
    pvs = [[_mm(e, kv_ref[0, :, D + hd * dh:D + (hd + 1) * dh]) for e, hd in zip(eg, hs)] for eg in es]
    os_ = [jnp.concatenate([o * (1.0 / jnp.sum(e, axis=-1, keepdims=True)) for o, e in zip(pg, eg)], axis=1)
           for pg, eg in zip(pvs, es)]
    y2s = [_mm(o, wo_ref[...]) for o in os_]
    for grp, h1, y2 in zip(groups, h1s, y2s):
        out = h1 + _rms(y2, g_ref[3:4, :])
        for i, v in enumerate(grp):
            h_ref[0, :, v * D:(v + 1) * D] = out[i * R:(i + 1) * R, :]


def _attn(x, o_gdn, c, norm_g, w_out, kv, w_q, w_o, *, tm):
    B, TP, DP = x.shape
    M = kv.shape[1]
    row_spec = lambda a: pl.BlockSpec((1, tm // PHASES, a.shape[2]), lambda b, i: (b, i, 0))
    return pl.pallas_call(
        functools.partial(_attn_kernel, n_split=2),
        grid=(B, TP * PHASES // tm),
        in_specs=[row_spec(x), row_spec(o_gdn), row_spec(c), _const_spec(norm_g.shape),
                  _const_spec(w_out.shape), pl.BlockSpec((1, M, kv.shape[2]), lambda b, i: (b, 0, 0)),
                  _const_spec(w_q.shape), _const_spec(w_o.shape)],
        out_specs=row_spec(x),
        out_shape=jax.ShapeDtypeStruct((B, TP, DP), F32),
        compiler_params=_params(2),
        name="attn",
    )(x, o_gdn, c, norm_g, w_out, kv, w_q, w_o)


def _ffn_kernel(h_ref, g_ref, wup_ref, cw_ref, cb_ref, wdn_ref, o_ref, carry, *, tm, fc, conv_k):
    F = wdn_ref.shape[0]
    D = wdn_ref.shape[1]
    n_chunks = F // fc
    R = tm // PHASES
    first_tail = PHASES - (conv_k - 1)

    @pl.when(pl.program_id(1) == 0)
    def _():
        carry[...] = jnp.zeros(carry.shape, F32)

    h2 = jnp.concatenate(_phase_slabs(h_ref, D), axis=0)
    hn = _rms(h2, g_ref[4:5, :]).astype(BF16)
    sub = lax.broadcasted_iota(jnp.int32, (1, SUBLANES, fc), 1)

    def up(j):
        return [jnp.dot(hn, wup_ref[:, lo:lo + fc], preferred_element_type=F32) for lo in (j * fc, F + j * fc)]

    def conv(u, lo):
        slabs = [u[v * R:(v + 1) * R, :] for v in range(PHASES)]
        tails = {w: carry[w - first_tail, :, lo:lo + fc] for w in range(first_tail, PHASES)}
        for w in range(first_tail, PHASES):
            carry[w - first_tail, :, lo:lo + fc] = slabs[w][R - SUBLANES:, :]
        taps = [cw_ref[k:k + 1, lo:lo + fc] for k in range(conv_k)]
        return jnp.concatenate(_phase_conv(slabs, tails, taps, sub, bias=cb_ref[:, lo:lo + fc]), axis=0)

    acc = None
    nxt = up(0)
    for j in range(n_chunks):
        ua, ub = nxt
        if j + 1 < n_chunks:
            nxt = up(j + 1)
        act = _silu(conv(ua, j * fc)) * conv(ub, F + j * fc)
        d = jnp.dot(act.astype(BF16), wdn_ref[j * fc:(j + 1) * fc, :], preferred_element_type=F32)
        acc = d if acc is None else acc + d
    out = h2 + _rms(acc, g_ref[5:6, :])
    o_ref[0] = pltpu.einshape("vjc->jvc", out.reshape(PHASES, R, D)).reshape(tm, D)


def _ffn(h, norm_g, w_up, conv_w, conv_b, w_down, *, tm, fc):
    B, TP, DP = h.shape
    T, D = TP * PHASES, DP // PHASES
    F = w_down.shape[0]
    conv_k = conv_w.shape[0]
    assert conv_k - 1 < PHASES and tm % (PHASES * SUBLANES) == 0
    row_spec = pl.BlockSpec((1, tm // PHASES, PHASES * D), lambda b, i: (b, i, 0))
    kernel = functools.partial(_ffn_kernel, tm=tm, fc=fc, conv_k=conv_k)
    return pl.pallas_call(
        kernel,
        grid=(B, T // tm),
        in_specs=[row_spec, _const_spec(norm_g.shape), _const_spec(w_up.shape), _const_spec(conv_w.shape),
                  _const_spec(conv_b.shape), _const_spec(w_down.shape)],
        out_specs=pl.BlockSpec((1, tm, D), lambda b, i: (b, i, 0)),
        out_shape=jax.ShapeDtypeStruct((B, T, D), F32),
        scratch_shapes=[pltpu.VMEM((conv_k - 1, SUBLANES, 2 * F), F32)],
        compiler_params=_params(2),
        name="ffn",
    )(h, norm_g, w_up, conv_w, conv_b, w_down)


def _pick_tile(T, pref):
    t = min(T, pref)
    assert T % t == 0 and t % CHUNK == 0, (T, t)
    return t


def _layer(h, mem, g, w_in, gdn_conv_w, a_log, dt_bias, gdn_norm_g, cfm_dw_w, cfm_dw_b, cfm_ln_g, cfm_ln_b,
           w_out, mem_norm_g, xa_w_q, xa_w_kv, xa_w_o, ffn_w_up, ffn_conv_w, ffn_conv_b, ffn_w_down):
    B, T, D = h.shape
    s2 = 4 * GDN_WIDTH
    s4 = s2 + 2 * GDN_HEADS
    w_qkvg = w_in[:, :s2].astype(BF16)
    w_ab = w_in[:, s2:s4].T.astype(BF16)
    w_cfm = w_in[:, s4:].astype(BF16)
    pad = jnp.zeros((GDN_HEADS,), F32)
    a_log8 = jnp.concatenate([a_log.astype(F32), pad])[:, None]
    dt_bias8 = jnp.concatenate([dt_bias.astype(F32), pad])[:, None]

    tm = _pick_tile(T, 512)
    kv = _mem_kv(mem, mem_norm_g[None, :], xa_w_kv.astype(BF16))
    q, k, v, gate, gb, c, hf = _mixer_in(h, g[0:1], w_qkvg, w_ab, w_cfm, gdn_conv_w, a_log8, dt_bias8,
                                         cfm_dw_w, cfm_dw_b[None, :], cfm_ln_g[None, :], cfm_ln_b[None, :], tm=tm)
    o_gdn = _gdn(q, k, v, gate, gb, gdn_norm_g[None, :], tg=_pick_tile(T, 4 * CHUNK))
    hf = _attn(hf, o_gdn, c, g, w_out.astype(BF16), kv, xa_w_q.astype(BF16), xa_w_o.astype(BF16), tm=tm)
    return _ffn(hf, g, ffn_w_up.astype(BF16), ffn_conv_w, ffn_conv_b[None, :], ffn_w_down.astype(BF16),
                tm=tm, fc=256)


def kernel(x, mem, norm_g, w_in, gdn_conv_w, gdn_a_log, gdn_dt_bias, gdn_norm_g, cfm_dw_w, cfm_dw_b,
           cfm_ln_g, cfm_ln_b, w_out, mem_norm_g, xa_w_q, xa_w_kv, xa_w_o, ffn_w_up, ffn_conv_w,
           ffn_conv_b, ffn_w_down):
    h = x
    for l in range(norm_g.shape[0]):
        h = _layer(h, mem, norm_g[l], w_in[l], gdn_conv_w[l], gdn_a_log[l], gdn_dt_bias[l], gdn_norm_g[l],
                   cfm_dw_w[l], cfm_dw_b[l], cfm_ln_g[l], cfm_ln_b[l], w_out[l], mem_norm_g[l],
                   xa_w_q[l], xa_w_kv[l], xa_w_o[l], ffn_w_up[l], ffn_conv_w[l], ffn_conv_b[l], ffn_w_down[l])
    return h
```

```python
import functools

import jax
import jax.numpy as jnp
from jax import lax
from jax.experimental import pallas as pl
from jax.experimental.pallas import tpu as pltpu

EPS = 1e-6
GDN_HEADS = 4
GDN_HEAD_DIM = 128
GDN_WIDTH = GDN_HEADS * GDN_HEAD_DIM
GDN_SHORT_CONV = 4
CHUNK = 64
XATTN_HEADS = 4

F32 = jnp.float32
BF16 = jnp.bfloat16

SUBLANES = 8
PHASES = SUBLANES
VMEM_LIMIT_BYTES = 56 * 1024 * 1024

NT_DIMS = (((1,), (1,)), ((), ()))
TN_DIMS = (((0,), (0,)), ((), ()))


def _mm(a, b):
    return jnp.dot(a.astype(BF16), b.astype(BF16), preferred_element_type=F32)


def _mm_nt(a, b):
    return lax.dot_general(a.astype(BF16), b.astype(BF16), NT_DIMS, preferred_element_type=F32)


def _mm_tn(a, b):
    return lax.dot_general(a.astype(BF16), b.astype(BF16), TN_DIMS, preferred_element_type=F32)


def _rms(x, g):
    return x * lax.rsqrt(jnp.mean(x * x, axis=-1, keepdims=True) + EPS) * g


NEG_LOG2_E = -1.4426950408889634


def _sigmoid(x):
    return 1.0 / (1.0 + jnp.exp2(x * NEG_LOG2_E))


def _silu(x):
    return x * _sigmoid(x)


def _delay_rows(x, tail, s, sub):
    rows, width = x.shape
    groups = jnp.concatenate([tail, x], axis=0).reshape(rows // SUBLANES + 1, SUBLANES, width)
    rot = pltpu.roll(groups, s, 1)
    return jnp.where(sub < s, rot[:-1], rot[1:]).reshape(rows, width)


def _phase_slabs(ref, width):
    return [ref[0, :, v * width:(v + 1) * width] for v in range(PHASES)]


def _phase_conv(slabs, tails, taps, sub, bias=None):
    K = len(taps)
    delayed = {w: _delay_rows(slabs[w], tails[w], 1, sub) for w in range(PHASES - (K - 1), PHASES)}
    outs = []
    for v in range(PHASES):
        acc = taps[K - 1] * slabs[v]
        if bias is not None:
            acc = acc + bias
        for d in range(1, K):
            acc = acc + taps[K - 1 - d] * (slabs[v - d] if v >= d else delayed[v - d + PHASES])
        outs.append(acc)
    return outs


def _const_spec(shape):
    zeros = (0,) * len(shape)
    return pl.BlockSpec(shape, lambda *_: zeros)


def _params(n_axes):
    return pltpu.CompilerParams(dimension_semantics=("arbitrary",) * n_axes,
                                vmem_limit_bytes=VMEM_LIMIT_BYTES)


def _mem_kv_kernel(mem_ref, g_ref, w_ref, kv_ref):
    mn = _rms(mem_ref[0], g_ref[...])
    kv_ref[0] = _mm(mn, w_ref[...]).astype(BF16)


def _mem_kv(mem, g, w_kv):
    B, M, D = mem.shape
    N = w_kv.shape[1]
    return pl.pallas_call(
        _mem_kv_kernel,
        grid=(B,),
        in_specs=[pl.BlockSpec((1, M, D), lambda b: (b, 0, 0)), _const_spec((1, D)), _const_spec((D, N))],
        out_specs=pl.BlockSpec((1, M, N), lambda b: (b, 0, 0)),
        out_shape=jax.ShapeDtypeStruct((B, M, N), BF16),
        compiler_params=_params(1),
        name="mem_kv",
    )(mem, g, w_kv)


def _mixer_in_kernel(x_ref, g_ref, wqkvg_ref, wab_ref, wcfm_ref, convw_ref, alog_ref, dtb_ref,
                     dww_ref, dwb_ref, lng_ref, lnb_ref,
                     q_ref, k_ref, v_ref, gate_ref, gb_ref, c_ref, xf_ref,
                     ztail, ctail, cdel, perm, *, tm, cfm_k):
    qkv_w = 3 * GDN_WIDTH
    W = GDN_WIDTH
    D = x_ref.shape[2]
    cw = c_ref.shape[2] // PHASES
    R = tm // PHASES
    kc = GDN_SHORT_CONV
    first_tail = PHASES - (kc - 1)

    @pl.when(pl.program_id(1) == 0)
    def _():
        ztail[...] = jnp.zeros(ztail.shape, F32)
        ctail[...] = jnp.zeros(ctail.shape, F32)

    @pl.when((pl.program_id(0) == 0) & (pl.program_id(1) == 0))
    def _():
        src = lax.broadcasted_iota(jnp.int32, (tm, tm), 0)
        dst = lax.broadcasted_iota(jnp.int32, (tm, tm), 1)
        g_, v_, s_ = dst // CHUNK, (dst % CHUNK) // SUBLANES, dst % SUBLANES
        perm[...] = jnp.where(src == v_ * R + g_ * SUBLANES + s_, 1.0, 0.0).astype(F32)

    h = pltpu.einshape("jvc->vjc", x_ref[0].reshape(R, PHASES, D)).reshape(tm, D)
    for v in range(PHASES):
        xf_ref[0, :, v * D:(v + 1) * D] = h[v * R:(v + 1) * R, :]
    hb = _rms(h, g_ref[...]).astype(BF16)

    z = jnp.dot(hb, wqkvg_ref[...], preferred_element_type=F32)
    ab = lax.dot_general(wab_ref[...], hb, NT_DIMS, preferred_element_type=F32)
    cz = jnp.dot(hb, wcfm_ref[...], preferred_element_type=F32)

    sub_z = lax.broadcasted_iota(jnp.int32, (1, SUBLANES, qkv_w), 1)
    zs = [z[v * R:(v + 1) * R, :qkv_w] for v in range(PHASES)]
    tails = {w: ztail[w - first_tail] for w in range(first_tail, PHASES)}
    for w in range(first_tail, PHASES):
        ztail[w - first_tail] = zs[w][R - SUBLANES:, :]
    taps = [convw_ref[j:j + 1, :] for j in range(kc)]
    q_scale = GDN_HEAD_DIM ** -0.5
    for v, acc in enumerate(_phase_conv(zs, tails, taps, sub_z)):
        qkv = _silu(acc)
        gate_ref[0, :, v * W:(v + 1) * W] = _silu(z[v * R:(v + 1) * R, qkv_w:])
        for hd in range(GDN_HEADS):
            lo = hd * GDN_HEAD_DIM
            qh = qkv[:, lo:lo + GDN_HEAD_DIM]
            kh = qkv[:, W + lo:W + lo + GDN_HEAD_DIM]
            q_ref[0, :, v * W + lo:v * W + lo + GDN_HEAD_DIM] = qh * (
                lax.rsqrt(jnp.sum(qh * qh, axis=-1, keepdims=True) + EPS) * q_scale)
            k_ref[0, :, v * W + lo:v * W + lo + GDN_HEAD_DIM] = kh * lax.rsqrt(
                jnp.sum(kh * kh, axis=-1, keepdims=True) + EPS)
        v_ref[0, :, v * W:(v + 1) * W] = qkv[:, 2 * W:]

    ab = jnp.dot(ab, perm[...], precision=lax.Precision.HIGHEST, preferred_element_type=F32)
    sp_in = ab + dtb_ref[...]
    softplus = jnp.maximum(sp_in, 0.0) + jnp.log(1.0 + jnp.exp(-jnp.abs(sp_in)))
    g = -jnp.exp(alog_ref[...]) * softplus
    row = lax.broadcasted_iota(jnp.int32, ab.shape, 0)
    gb = jnp.where(row < GDN_HEADS, g, _sigmoid(ab))
    for c in range(tm // CHUNK):
        gb_ref[0, c] = gb[:, c * CHUNK:(c + 1) * CHUNK]

    sub_c = lax.broadcasted_iota(jnp.int32, (1, SUBLANES, cw), 1)
    n_del = cdel.shape[0]
    for w in range(PHASES):
        glu = cz[w * R:(w + 1) * R, :cw] * _sigmoid(cz[w * R:(w + 1) * R, cw:])
        tail = ctail[w]
        ctail[w] = glu[R - SUBLANES:, :]
        cdel[0, w * R:(w + 1) * R, :] = glu
        for m in range(1, n_del):
            cdel[m, w * R:(w + 1) * R, :] = _delay_rows(glu, tail, m, sub_c)
    convs = []
    for v in range(PHASES):
        acc = dwb_ref[...]
        for d in range(cfm_k):
            a, e = divmod(d, PHASES)
            w, m = (v - e) % PHASES, a + (1 if v < e else 0)
            acc = acc + dww_ref[cfm_k - 1 - d:cfm_k - d, :] * cdel[m, w * R:(w + 1) * R, :]
        convs.append(acc)
    conv = jnp.concatenate(convs, axis=0)
    mu = jnp.mean(conv, axis=-1, keepdims=True)
    xc = conv - mu
    var = jnp.mean(xc * xc, axis=-1, keepdims=True)
    cout = _silu(xc * lax.rsqrt(var + EPS) * lng_ref[...] + lnb_ref[...])
    for v in range(PHASES):
        c_ref[0, :, v * cw:(v + 1) * cw] = cout[v * R:(v + 1) * R, :]


def _mixer_in(x, g0, w_qkvg, w_ab, w_cfm, conv_w, a_log8, dt_bias8, dw_w, dw_b, ln_g, ln_b, *, tm):
    B, T, D = x.shape
    cfm_k, cw = dw_w.shape
    assert GDN_SHORT_CONV - 1 < PHASES and tm % (PHASES * SUBLANES) == 0
    n_del = (cfm_k - 1) // PHASES + 2
    assert n_del - 1 < SUBLANES
    R = tm // PHASES
    row_spec = lambda w: pl.BlockSpec((1, R, PHASES * w), lambda b, i: (b, i, 0))
    kernel = functools.partial(_mixer_in_kernel, tm=tm, cfm_k=cfm_k)
    wide = jax.ShapeDtypeStruct((B, T // PHASES, PHASES * GDN_WIDTH), F32)
    return pl.pallas_call(
        kernel,
        grid=(B, T // tm),
        in_specs=[pl.BlockSpec((1, tm, D), lambda b, i: (b, i, 0)), _const_spec(g0.shape), _const_spec(w_qkvg.shape), _const_spec(w_ab.shape),
                  _const_spec(w_cfm.shape), _const_spec(conv_w.shape), _const_spec(a_log8.shape),
                  _const_spec(dt_bias8.shape), _const_spec(dw_w.shape), _const_spec(dw_b.shape),
                  _const_spec(ln_g.shape), _const_spec(ln_b.shape)],
        out_specs=[row_spec(GDN_WIDTH), row_spec(GDN_WIDTH), row_spec(GDN_WIDTH), row_spec(GDN_WIDTH),
                   pl.BlockSpec((1, tm // CHUNK, 2 * GDN_HEADS, CHUNK), lambda b, i: (b, i, 0, 0)),
                   row_spec(cw), row_spec(D)],
        out_shape=[wide, wide, wide, wide,
                   jax.ShapeDtypeStruct((B, T // CHUNK, 2 * GDN_HEADS, CHUNK), F32),
                   jax.ShapeDtypeStruct((B, T // PHASES, PHASES * cw), F32),
                   jax.ShapeDtypeStruct((B, T // PHASES, PHASES * D), F32)],
        scratch_shapes=[pltpu.VMEM((GDN_SHORT_CONV - 1, SUBLANES, 3 * GDN_WIDTH), F32),
                        pltpu.VMEM((PHASES, SUBLANES, cw), F32),
                        pltpu.VMEM((n_del, tm, cw), F32),
                        pltpu.VMEM((tm, tm), F32)],
        compiler_params=_params(2),
        name="mixer_in",
    )(x, g0, w_qkvg, w_ab, w_cfm, conv_w, a_log8, dt_bias8, dw_w, dw_b, ln_g, ln_b)


def _block_diag(x4, lane_masks):
    xb = x4.astype(BF16)
    return jnp.concatenate([xb * m for m in lane_masks], axis=0)


def _packed_lower_inverse(Ls, eye, diag_mask, off_masks, lane_masks):
    mmp = lambda a, bd: jnp.dot(a.astype(BF16), bd, preferred_element_type=F32)
    Ps = [jnp.where(diag_mask, L, 0.0) for L in Ls]
    Ts = [eye - P for P in Ps]
    for _ in range(3):
        Ps = [mmp(P, _block_diag(P, lane_masks)) for P in Ps]
        Bs = [_block_diag(P, lane_masks) for P in Ps]
        Ts = [T + mmp(T, B) for T, B in zip(Ts, Bs)]
    for m in off_masks:
        Xs = [mmp(T, _block_diag(jnp.where(m, L, 0.0), lane_masks)) for T, L in zip(Ts, Ls)]
        Ts = [T - mmp(X, _block_diag(T, lane_masks)) for T, X in zip(Ts, Xs)]
    return Ts


def _gdn_kernel(q_ref, k_ref, v_ref, gate_ref, gb_ref, ng_ref, o_ref, s_ref, *, n_chunks):
    @pl.when(pl.program_id(0) == 0)
    def _():
        s_ref[...] = jnp.zeros(s_ref.shape, F32)

    C, H, Dh = CHUNK, GDN_HEADS, GDN_HEAD_DIM
    W = H * Dh
    rows_per_chunk = C // PHASES
    nb = q_ref.shape[0]
    ri = lax.broadcasted_iota(jnp.int32, (C, H * C), 0)
    li = lax.broadcasted_iota(jnp.int32, (C, H * C), 1)
    lb, ci = li // C, li % C
    ti = PHASES * (ri % rows_per_chunk) + ri // rows_per_chunk
    tj = PHASES * (ci % rows_per_chunk) + ci // rows_per_chunk
    causal = ti >= tj
    strict = ti > tj
    eye = jnp.where(ri == ci, 1.0, 0.0).astype(F32)
    diag_mask = (ti // 16) == (tj // 16)
    off_masks = [((ti // (2 * s)) == (tj // (2 * s))) & ((ti // s) > (tj // s)) for s in (16, 32)]
    lane_masks = [jnp.where(lb == u, 1.0, 0.0).astype(BF16) for u in range(H)]
    lower1 = jnp.where(causal[:, :C], 1.0, 0.0).astype(F32)
    upper4 = jnp.where(ti <= tj, 1.0, 0.0).astype(F32)
    eye1 = eye[:, :C]
    lb_row = lb[0:1, :]
    hi = lax.Precision.HIGHEST
    ng = ng_ref[...]
    zero_k = jnp.zeros((C, Dh), BF16)

    def load(ref, b, c, h):
        r0 = c * rows_per_chunk
        return jnp.concatenate([ref[b, r0:r0 + rows_per_chunk, p * W + h * Dh:p * W + (h + 1) * Dh]
                                for p in range(PHASES)], axis=0)

    def pick_head(parts):
        out = parts[H - 1]
        for u in range(H - 2, -1, -1):
            out = jnp.where(lb == u, parts[u], out)
        return out

    groups = [(c, b) for c in range(n_chunks) for b in range(nb)]
    A4s, KKs, decays, rhss, qgss, kdecss, elasts = [], [], [], [], [], [], []
    for c, b in groups:
        gbc = gb_ref[b, c]
        gc_rows = jnp.dot(gbc, upper4, precision=hi, preferred_element_type=F32)
        gc_col = lax.dot_general(lower1, gbc, NT_DIMS, precision=hi, preferred_element_type=F32)
        gb_col = lax.dot_general(eye1, gbc, NT_DIMS, precision=hi, preferred_element_type=F32)
        grow4 = gc_rows[H - 1:H, :]
        for u in range(H - 2, -1, -1):
            grow4 = jnp.where(lb_row == u, gc_rows[u:u + 1, :], grow4)
        gcol4 = pick_head([gc_col[:, u:u + 1] for u in range(H)])
        decays.append(jnp.where(causal, jnp.exp(jnp.where(causal, gcol4 - grow4, 0.0)), 0.0))
        kbqs, kks, rhs, qgs, kdecs, els = [], [], [], [], [], []
        for h in range(H):
            qh = load(q_ref, b, c, h)
            kh = load(k_ref, b, c, h)
            vh = load(v_ref, b, c, h)
            gcol = gc_col[:, h:h + 1]
            bcol = gb_col[:, H + h:H + h + 1]
            g_last = gc_rows[h:h + 1, C - 1:C]
            egc = jnp.exp(gcol)
            kb = kh * bcol
            kbqs.append(jnp.concatenate([kb, qh], axis=0).astype(BF16))
            kb16 = kh.astype(BF16)
            kks.append(jnp.concatenate([kb16 if u == h else zero_k for u in range(H)], axis=1))
            rhs.append(jnp.concatenate([vh * bcol, kb * egc], axis=1).astype(BF16))
            qgs.append(qh * egc)
            kdecs.append(kh * jnp.exp(g_last - gcol))
            els.append(jnp.exp(g_last))
        A4s.append(jnp.concatenate(kbqs, axis=1))
        KKs.append(jnp.concatenate(kks, axis=0))
        rhss.append(jnp.concatenate(rhs, axis=0))
        qgss.append(qgs)
        kdecss.append(kdecs)
        elasts.append(els)
    prods = [lax.dot_general(a, kk, NT_DIMS, preferred_element_type=F32) for a, kk in zip(A4s, KKs)]
    Ls = [jnp.where(strict, p[:C] * d, 0.0) for p, d in zip(prods, decays)]
    attns = [_block_diag(p[C:] * d, lane_masks) for p, d in zip(prods, decays)]
    Ts = _packed_lower_inverse(Ls, eye, diag_mask, off_masks, lane_masks)
    sols = [jnp.dot(_block_diag(T, lane_masks), r, preferred_element_type=F32) for T, r in zip(Ts, rhss)]

    for c in range(n_chunks):
        gi = [c * nb + b for b in range(nb)]
        Ss = [[s_ref[b * H + h] for h in range(H)] for b in range(nb)]
        ws_qs = [[_mm(jnp.concatenate([sols[g][h * C:(h + 1) * C, Dh:], qgss[g][h]], axis=0), Ss[b][h])
                  for h in range(H)] for b, g in enumerate(gi)]
        v_new = [[(sols[g][h * C:(h + 1) * C, :Dh] - ws_qs[b][h][:C]).astype(BF16) for h in range(H)]
                 for b, g in enumerate(gi)]
        o_att = [jnp.dot(attns[g], jnp.concatenate(v_new[b], axis=0), preferred_element_type=F32)
                 for b, g in enumerate(gi)]
        for b, g in enumerate(gi):
            for h in range(H):
                s_ref[b * H + h] = Ss[b][h] * elasts[g][h] + _mm_tn(kdecss[g][h], v_new[b][h])
        r0 = c * rows_per_chunk
        for b, g in enumerate(gi):
            for h in range(H):
                o = ws_qs[b][h][C:] + o_att[b][h * C:(h + 1) * C]
                on = _rms(o, ng) * load(gate_ref, b, c, h)
                for p in range(PHASES):
                    o_ref[b, r0:r0 + rows_per_chunk, p * W + h * Dh:p * W + (h + 1) * Dh] = (
                        on[p * rows_per_chunk:(p + 1) * rows_per_chunk])


def _gdn(q, k, v, gate, gb, norm_g, *, tg):
    B, TP, WP = q.shape
    n_chunks = tg // CHUNK
    row_spec = pl.BlockSpec((B, tg // PHASES, WP), lambda i: (0, i, 0))
    return pl.pallas_call(
        functools.partial(_gdn_kernel, n_chunks=n_chunks),
        grid=(TP * PHASES // tg,),
        in_specs=[row_spec, row_spec, row_spec, row_spec,
                  pl.BlockSpec((B, n_chunks, 2 * GDN_HEADS, CHUNK), lambda i: (0, i, 0, 0)),
                  _const_spec(norm_g.shape)],
        out_specs=row_spec,
        out_shape=jax.ShapeDtypeStruct((B, TP, WP), F32),
        scratch_shapes=[pltpu.VMEM((B * GDN_HEADS, GDN_HEAD_DIM, GDN_HEAD_DIM), F32)],
        compiler_params=_params(1),
        name="gdn",
    )(q, k, v, gate, gb, norm_g)


def _attn_kernel(x_ref, og_ref, c_ref, g_ref, wout_ref, kv_ref, wq_ref, wo_ref, h_ref, *, n_split):
    D = x_ref.shape[2] // PHASES
    R = x_ref.shape[1]
    dh = D // XATTN_HEADS
    per = PHASES // n_split
    groups = [range(i * per, (i + 1) * per) for i in range(n_split)]
    hs = range(XATTN_HEADS)

    def rows(ref, grp):
        w = ref.shape[2] // PHASES
        return jnp.concatenate([ref[0, :, v * w:(v + 1) * w] for v in grp], axis=0)

    mixes = [jnp.concatenate([rows(og_ref, grp).astype(BF16), rows(c_ref, grp).astype(BF16)], axis=1)
             for grp in groups]
    ys = [jnp.dot(m, wout_ref[...], preferred_element_type=F32) for m in mixes]
    h1s = [rows(x_ref, grp) + _rms(y, g_ref[1:2, :]) for grp, y in zip(groups, ys)]
    hns = [_rms(h1, g_ref[2:3, :]).astype(BF16) for h1 in h1s]
    qs = [(jnp.dot(hn, wq_ref[...], preferred_element_type=F32) * (dh ** -0.5)).astype(BF16) for hn in hns]
    ss = [[_mm_nt(q[:, hd * dh:(hd + 1) * dh], kv_ref[0, :, hd * dh:(hd + 1) * dh]) for hd in hs] for q in qs]
    es = [[jnp.exp(s - jnp.max(s, axis=-1, keepdims=True)) for s in sg] for sg in ss]
    pvs = [[_mm(e, kv_ref[0, :, D + hd * dh:D + (hd + 1) * dh]) for e, hd in zip(eg, hs)] for eg in es]
    os_ = [jnp.concatenate([o * (1.0 / jnp.sum(e, axis=-1, keepdims=True)) for o, e in zip(pg, eg)], axis=1)
           for pg, eg in zip(pvs, es)]
    y2s = [_mm(o, wo_ref[...]) for o in os_]
    for grp, h1, y2 in zip(groups, h1s, y2s):
        out = h1 + _rms(y2, g_ref[3:4, :])
        for i, v in enumerate(grp):
            h_ref[0, :, v * D:(v + 1) * D] = out[i * R:(i + 1) * R, :]


def _attn(x, o_gdn, c, norm_g, w_out, kv, w_q, w_o, *, tm):
    B, TP, DP = x.shape
    M = kv.shape[1]
    row_spec = lambda a: pl.BlockSpec((1, tm // PHASES, a.shape[2]), lambda b, i: (b, i, 0))
    return pl.pallas_call(
        functools.partial(_attn_kernel, n_split=2),
        grid=(B, TP * PHASES // tm),
        in_specs=[row_spec(x), row_spec(o_gdn), row_spec(c), _const_spec(norm_g.shape),
                  _const_spec(w_out.shape), pl.BlockSpec((1, M, kv.shape[2]), lambda b, i: (b, 0, 0)),
                  _const_spec(w_q.shape), _const_spec(w_o.shape)],
        out_specs=row_spec(x),
        out_shape=jax.ShapeDtypeStruct((B, TP, DP), F32),
        compiler_params=_params(2),
        name="attn",
    )(x, o_gdn, c, norm_g, w_out, kv, w_q, w_o)


def _ffn_kernel(h_ref, g_ref, wup_ref, cw_ref, cb_ref, wdn_ref, o_ref, carry, *, tm, fc, conv_k):
    F = wdn_ref.shape[0]
    D = wdn_ref.shape[1]
    n_chunks = F // fc
    R = tm // PHASES
    first_tail = PHASES - (conv_k - 1)

    @pl.when(pl.program_id(1) == 0)
    def _():
        carry[...] = jnp.zeros(carry.shape, F32)

    h2 = jnp.concatenate(_phase_slabs(h_ref, D), axis=0)
    hn = _rms(h2, g_ref[4:5, :]).astype(BF16)
    sub = lax.broadcasted_iota(jnp.int32, (1, SUBLANES, fc), 1)

    def up(j):
        return [jnp.dot(hn, wup_ref[:, lo:lo + fc], preferred_element_type=F32) for lo in (j * fc, F + j * fc)]

    def conv(u, lo):
        slabs = [u[v * R:(v + 1) * R, :] for v in range(PHASES)]
        tails = {w: carry[w - first_tail, :, lo:lo + fc] for w in range(first_tail, PHASES)}
        for w in range(first_tail, PHASES):
            carry[w - first_tail, :, lo:lo + fc] = slabs[w][R - SUBLANES:, :]
        taps = [cw_ref[k:k + 1, lo:lo + fc] for k in range(conv_k)]
        return jnp.concatenate(_phase_conv(slabs, tails, taps, sub, bias=cb_ref[:, lo:lo + fc]), axis=0)

    acts = []
    nxt = up(0)
    for j in range(n_chunks):
        ua, ub = nxt
        if j + 1 < n_chunks:
            nxt = up(j + 1)
        acts.append((_silu(conv(ua, j * fc)) * conv(ub, F + j * fc)).astype(BF16))
    acc = jnp.dot(jnp.concatenate(acts, axis=1), wdn_ref[...], preferred_element_type=F32)
    out = h2 + _rms(acc, g_ref[5:6, :])
    o_ref[0] = pltpu.einshape("vjc->jvc", out.reshape(PHASES, R, D)).reshape(tm, D)


def _ffn(h, norm_g, w_up, conv_w, conv_b, w_down, *, tm, fc):
    B, TP, DP = h.shape
    T, D = TP * PHASES, DP // PHASES
    F = w_down.shape[0]
    conv_k = conv_w.shape[0]
    assert conv_k - 1 < PHASES and tm % (PHASES * SUBLANES) == 0
    row_spec = pl.BlockSpec((1, tm // PHASES, PHASES * D), lambda b, i: (b, i, 0))
    kernel = functools.partial(_ffn_kernel, tm=tm, fc=fc, conv_k=conv_k)
    return pl.pallas_call(
        kernel,
        grid=(B, T // tm),
        in_specs=[row_spec, _const_spec(norm_g.shape), _const_spec(w_up.shape), _const_spec(conv_w.shape),
                  _const_spec(conv_b.shape), _const_spec(w_down.shape)],
        out_specs=pl.BlockSpec((1, tm, D), lambda b, i: (b, i, 0)),
        out_shape=jax.ShapeDtypeStruct((B, T, D), F32),
        scratch_shapes=[pltpu.VMEM((conv_k - 1, SUBLANES, 2 * F), F32)],
        compiler_params=_params(2),
        name="ffn",
    )(h, norm_g, w_up, conv_w, conv_b, w_down)


def _pick_tile(T, pref):
    t = min(T, pref)
    assert T % t == 0 and t % CHUNK == 0, (T, t)
    return t


def _layer(h, mem, g, w_in, gdn_conv_w, a_log, dt_bias, gdn_norm_g, cfm_dw_w, cfm_dw_b, cfm_ln_g, cfm_ln_b,
           w_out, mem_norm_g, xa_w_q, xa_w_kv, xa_w_o, ffn_w_up, ffn_conv_w, ffn_conv_b, ffn_w_down):
    B, T, D = h.shape
    s2 = 4 * GDN_WIDTH
    s4 = s2 + 2 * GDN_HEADS
    w_qkvg = w_in[:, :s2].astype(BF16)
    w_ab = w_in[:, s2:s4].T.astype(BF16)
    w_cfm = w_in[:, s4:].astype(BF16)
    pad = jnp.zeros((GDN_HEADS,), F32)
    a_log8 = jnp.concatenate([a_log.astype(F32), pad])[:, None]
    dt_bias8 = jnp.concatenate([dt_bias.astype(F32), pad])[:, None]

    tm = _pick_tile(T, 512)
    kv = _mem_kv(mem, mem_norm_g[None, :], xa_w_kv.astype(BF16))
    q, k, v, gate, gb, c, hf = _mixer_in(h, g[0:1], w_qkvg, w_ab, w_cfm, gdn_conv_w, a_log8, dt_bias8,
                                         cfm_dw_w, cfm_dw_b[None, :], cfm_ln_g[None, :], cfm_ln_b[None, :], tm=tm)
    o_gdn = _gdn(q, k, v, gate, gb, gdn_norm_g[None, :], tg=_pick_tile(T, 4 * CHUNK))
    hf = _attn(hf, o_gdn, c, g, w_out.astype(BF16), kv, xa_w_q.astype(BF16), xa_w_o.astype(BF16), tm=tm)
    return _ffn(hf, g, ffn_w_up.astype(BF16), ffn_conv_w, ffn_conv_b[None, :], ffn_w_down.astype(BF16),
                tm=tm, fc=256)


def kernel(x, mem, norm_g, w_in, gdn_conv_w, gdn_a_log, gdn_dt_bias, gdn_norm_g, cfm_dw_w, cfm_dw_b,
           cfm_ln_g, cfm_ln_b, w_out, mem_norm_g, xa_w_q, xa_w_kv, xa_w_o, ffn_w_up, ffn_conv_w,
           ffn_conv_b, ffn_w_down):
    h = x
    for l in range(norm_g.shape[0]):
        h = _layer(h, mem, norm_g[l], w_in[l], gdn_conv_w[l], gdn_a_log[l], gdn_dt_bias[l], gdn_norm_g[l],
                   cfm_dw_w[l], cfm_dw_b[l], cfm_ln_g[l], cfm_ln_b[l], w_out[l], mem_norm_g[l],
                   xa_w_q[l], xa_w_kv[l], xa_w_o[l], ffn_w_up[l], ffn_conv_w[l], ffn_conv_b[l], ffn_w_down[l])
    return h
```

```python
import functools

import jax
import jax.numpy as jnp
from jax import lax
from jax.experimental import pallas as pl
from jax.experimental.pallas import tpu as pltpu

EPS = 1e-6
GDN_HEADS = 4
GDN_HEAD_DIM = 128
GDN_WIDTH = GDN_HEADS * GDN_HEAD_DIM
GDN_SHORT_CONV = 4
CHUNK = 64
XATTN_HEADS = 4

F32 = jnp.float32
BF16 = jnp.bfloat16

SUBLANES = 8
PHASES = SUBLANES
VMEM_LIMIT_BYTES = 56 * 1024 * 1024

NT_DIMS = (((1,), (1,)), ((), ()))
TN_DIMS = (((0,), (0,)), ((), ()))


def _mm(a, b):
    return jnp.dot(a.astype(BF16), b.astype(BF16), preferred_element_type=F32)


def _mm_nt(a, b):
    return lax.dot_general(a.astype(BF16), b.astype(BF16), NT_DIMS, preferred_element_type=F32)


def _mm_tn(a, b):
    return lax.dot_general(a.astype(BF16), b.astype(BF16), TN_DIMS, preferred_element_type=F32)


def _rms(x, g):
    return x * lax.rsqrt(jnp.mean(x * x, axis=-1, keepdims=True) + EPS) * g


NEG_LOG2_E = -1.4426950408889634


def _sigmoid(x):
    return 1.0 / (1.0 + jnp.exp2(x * NEG_LOG2_E))


def _silu(x):
    return x * _sigmoid(x)


def _delay_rows(x, tail, s, sub):
    rows, width = x.shape
    groups = jnp.concatenate([tail, x], axis=0).reshape(rows // SUBLANES + 1, SUBLANES, width)
    rot = pltpu.roll(groups, s, 1)
    return jnp.where(sub < s, rot[:-1], rot[1:]).reshape(rows, width)


def _phase_slabs(ref, width):
    return [ref[0, :, v * width:(v + 1) * width] for v in range(PHASES)]


def _phase_conv(slabs, tails, taps, sub, bias=None):
    K = len(taps)
    delayed = {w: _delay_rows(slabs[w], tails[w], 1, sub) for w in range(PHASES - (K - 1), PHASES)}
    outs = []
    for v in range(PHASES):
        acc = taps[K - 1] * slabs[v]
        if bias is not None:
            acc = acc + bias
        for d in range(1, K):
            acc = acc + taps[K - 1 - d] * (slabs[v - d] if v >= d else delayed[v - d + PHASES])
        outs.append(acc)
    return outs


def _const_spec(shape):
    zeros = (0,) * len(shape)
    return pl.BlockSpec(shape, lambda *_: zeros)


def _params(n_axes):
    return pltpu.CompilerParams(dimension_semantics=("arbitrary",) * n_axes,
                                vmem_limit_bytes=VMEM_LIMIT_BYTES)


def _mem_kv_kernel(mem_ref, g_ref, w_ref, kv_ref):
    mn = _rms(mem_ref[0], g_ref[...])
    kv_ref[0] = _mm(mn, w_ref[...]).astype(BF16)


def _mem_kv(mem, g, w_kv):
    B, M, D = mem.shape
    N = w_kv.shape[1]
    return pl.pallas_call(
        _mem_kv_kernel,
        grid=(B,),
        in_specs=[pl.BlockSpec((1, M, D), lambda b: (b, 0, 0)), _const_spec((1, D)), _const_spec((D, N))],
        out_specs=pl.BlockSpec((1, M, N), lambda b: (b, 0, 0)),
        out_shape=jax.ShapeDtypeStruct((B, M, N), BF16),
        compiler_params=_params(1),
        name="mem_kv",
    )(mem, g, w_kv)


def _mixer_in_kernel(x_ref, g_ref, wqkvg_ref, wab_ref, wcfm_ref, convw_ref, alog_ref, dtb_ref,
                     dww_ref, dwb_ref, lng_ref, lnb_ref,
                     q_ref, k_ref, v_ref, gate_ref, gb_ref, c_ref, xf_ref,
                     ztail, ctail, cdel, perm, *, tm, cfm_k):
    qkv_w = 3 * GDN_WIDTH
    W = GDN_WIDTH
    D = x_ref.shape[2]
    cw = c_ref.shape[2] // PHASES
    n_split = cdel.shape[0]
    R = tm // PHASES
    Rg = R // n_split
    tg = tm // n_split
    kc = GDN_SHORT_CONV
    first_tail = PHASES - (kc - 1)
    n_del = cdel.shape[1]
    q_scale = GDN_HEAD_DIM ** -0.5

    @pl.when(pl.program_id(1) == 0)
    def _():
        ztail[...] = jnp.zeros(ztail.shape, F32)
        ctail[...] = jnp.zeros(ctail.shape, F32)

    @pl.when((pl.program_id(0) == 0) & (pl.program_id(1) == 0))
    def _():
        src = lax.broadcasted_iota(jnp.int32, (tg, tg), 0)
        dst = lax.broadcasted_iota(jnp.int32, (tg, tg), 1)
        g_, v_, s_ = dst // CHUNK, (dst % CHUNK) // SUBLANES, dst % SUBLANES
        perm[...] = jnp.where(src == v_ * Rg + g_ * SUBLANES + s_, 1.0, 0.0).astype(F32)

    h3 = pltpu.einshape("jvc->vjc", x_ref[0].reshape(R, PHASES, D))
    for v in range(PHASES):
        xf_ref[0, :, v * D:(v + 1) * D] = h3[v]
    groups = range(n_split)
    rows = [slice(i * Rg, (i + 1) * Rg) for i in groups]
    hbs = [_rms(h3[:, rows[i], :].reshape(tg, D), g_ref[...]).astype(BF16) for i in groups]

    czs = [jnp.dot(hb, wcfm_ref[...], preferred_element_type=F32) for hb in hbs]
    abs_ = [lax.dot_general(wab_ref[...], hb, NT_DIMS, preferred_element_type=F32) for hb in hbs]
    zs_ = [jnp.dot(hb, wqkvg_ref[...], preferred_element_type=F32) for hb in hbs]

    sub_c = lax.broadcasted_iota(jnp.int32, (1, SUBLANES, cw), 1)
    ctails = [ctail[w] for w in range(PHASES)]
    for i in groups:
        for w in range(PHASES):
            glu = czs[i][w * Rg:(w + 1) * Rg, :cw] * _sigmoid(czs[i][w * Rg:(w + 1) * Rg, cw:])
            cdel[i, 0, w * Rg:(w + 1) * Rg, :] = glu
            for m in range(1, n_del):
                cdel[i, m, w * Rg:(w + 1) * Rg, :] = _delay_rows(glu, ctails[w], m, sub_c)
            ctails[w] = glu[Rg - SUBLANES:, :]
    for w in range(PHASES):
        ctail[w] = ctails[w]
    for i in groups:
        for v in range(PHASES):
            acc = dwb_ref[...]
            for d in range(cfm_k):
                a, e = divmod(d, PHASES)
                w, m = (v - e) % PHASES, a + (1 if v < e else 0)
                acc = acc + dww_ref[cfm_k - 1 - d:cfm_k - d, :] * cdel[i, m, w * Rg:(w + 1) * Rg, :]
            mu = jnp.mean(acc, axis=-1, keepdims=True)
            xc = acc - mu
            var = jnp.mean(xc * xc, axis=-1, keepdims=True)
            c_ref[0, rows[i], v * cw:(v + 1) * cw] = _silu(xc * lax.rsqrt(var + EPS) * lng_ref[...] + lnb_ref[...])

    for i in groups:
        ab = jnp.dot(abs_[i], perm[...], precision=lax.Precision.HIGHEST, preferred_element_type=F32)
        sp_in = ab + dtb_ref[...]
        softplus = jnp.maximum(sp_in, 0.0) + jnp.log(1.0 + jnp.exp(-jnp.abs(sp_in)))
        g = -jnp.exp(alog_ref[...]) * softplus
        row = lax.broadcasted_iota(jnp.int32, ab.shape, 0)
        gb = jnp.where(row < GDN_HEADS, g, _sigmoid(ab))
        for c in range(tg // CHUNK):
            gb_ref[0, i * (tg // CHUNK) + c] = gb[:, c * CHUNK:(c + 1) * CHUNK]

    sub_z = lax.broadcasted_iota(jnp.int32, (1, SUBLANES, qkv_w), 1)
    taps = [convw_ref[j:j + 1, :] for j in range(kc)]
    tails = {w: ztail[w - first_tail] for w in range(first_tail, PHASES)}
    for i in groups:
        z = zs_[i]
        zs = [z[v * Rg:(v + 1) * Rg, :qkv_w] for v in range(PHASES)]
        for v, acc in enumerate(_phase_conv(zs, tails, taps, sub_z)):
            qkv = _silu(acc)
            gate_ref[0, rows[i], v * W:(v + 1) * W] = _silu(z[v * Rg:(v + 1) * Rg, qkv_w:])
            for hd in range(GDN_HEADS):
                lo = hd * GDN_HEAD_DIM
                qh = qkv[:, lo:lo + GDN_HEAD_DIM]
                kh = qkv[:, W + lo:W + lo + GDN_HEAD_DIM]
                q_ref[0, rows[i], v * W + lo:v * W + lo + GDN_HEAD_DIM] = qh * (
                    lax.rsqrt(jnp.sum(qh * qh, axis=-1, keepdims=True) + EPS) * q_scale)
                k_ref[0, rows[i], v * W + lo:v * W + lo + GDN_HEAD_DIM] = kh * lax.rsqrt(
                    jnp.sum(kh * kh, axis=-1, keepdims=True) + EPS)
            v_ref[0, rows[i], v * W:(v + 1) * W] = qkv[:, 2 * W:]
        tails = {w: zs[w][Rg - SUBLANES:, :] for w in range(first_tail, PHASES)}
    for w in range(first_tail, PHASES):
        ztail[w - first_tail] = tails[w]


def _mixer_in(x, g0, w_qkvg, w_ab, w_cfm, conv_w, a_log8, dt_bias8, dw_w, dw_b, ln_g, ln_b, *, tm):
    B, T, D = x.shape
    cfm_k, cw = dw_w.shape
    n_split = 2
    assert GDN_SHORT_CONV - 1 < PHASES and tm % (n_split * CHUNK) == 0
    n_del = (cfm_k - 1) // PHASES + 2
    assert n_del - 1 < SUBLANES
    R = tm // PHASES
    row_spec = lambda w: pl.BlockSpec((1, R, PHASES * w), lambda b, i: (b, i, 0))
    kernel = functools.partial(_mixer_in_kernel, tm=tm, cfm_k=cfm_k)
    wide = jax.ShapeDtypeStruct((B, T // PHASES, PHASES * GDN_WIDTH), F32)
    return pl.pallas_call(
        kernel,
        grid=(B, T // tm),
        in_specs=[pl.BlockSpec((1, tm, D), lambda b, i: (b, i, 0)), _const_spec(g0.shape), _const_spec(w_qkvg.shape), _const_spec(w_ab.shape),
                  _const_spec(w_cfm.shape), _const_spec(conv_w.shape), _const_spec(a_log8.shape),
                  _const_spec(dt_bias8.shape), _const_spec(dw_w.shape), _const_spec(dw_b.shape),
                  _const_spec(ln_g.shape), _const_spec(ln_b.shape)],
        out_specs=[row_spec(GDN_WIDTH), row_spec(GDN_WIDTH), row_spec(GDN_WIDTH), row_spec(GDN_WIDTH),
                   pl.BlockSpec((1, tm // CHUNK, 2 * GDN_HEADS, CHUNK), lambda b, i: (b, i, 0, 0)),
                   row_spec(cw), row_spec(D)],
        out_shape=[wide, wide, wide, wide,
                   jax.ShapeDtypeStruct((B, T // CHUNK, 2 * GDN_HEADS, CHUNK), F32),
                   jax.ShapeDtypeStruct((B, T // PHASES, PHASES * cw), F32),
                   jax.ShapeDtypeStruct((B, T // PHASES, PHASES * D), F32)],
        scratch_shapes=[pltpu.VMEM((GDN_SHORT_CONV - 1, SUBLANES, 3 * GDN_WIDTH), F32),
                        pltpu.VMEM((PHASES, SUBLANES, cw), F32),
                        pltpu.VMEM((n_split, n_del, tm // n_split, cw), F32),
                        pltpu.VMEM((tm // n_split, tm // n_split), F32)],
        compiler_params=_params(2),
        name="mixer_in",
    )(x, g0, w_qkvg, w_ab, w_cfm, conv_w, a_log8, dt_bias8, dw_w, dw_b, ln_g, ln_b)


def _block_diag(x4, lane_masks):
    xb = x4.astype(BF16)
    return jnp.concatenate([xb * m for m in lane_masks], axis=0)


def _packed_lower_inverse(Ls, eye, diag_mask, off_masks, lane_masks):
    mmp = lambda a, bd: jnp.dot(a.astype(BF16), bd, preferred_element_type=F32)
    Ps = [jnp.where(diag_mask, L, 0.0) for L in Ls]
    Ts = [eye - P for P in Ps]
    for _ in range(3):
        Ps = [mmp(P, _block_diag(P, lane_masks)) for P in Ps]
        Bs = [_block_diag(P, lane_masks) for P in Ps]
        Ts = [T + mmp(T, B) for T, B in zip(Ts, Bs)]
    for m in off_masks:
        Xs = [mmp(T, _block_diag(jnp.where(m, L, 0.0), lane_masks)) for T, L in zip(Ts, Ls)]
        Ts = [T - mmp(X, _block_diag(T, lane_masks)) for T, X in zip(Ts, Xs)]
    return Ts


def _gdn_kernel(q_ref, k_ref, v_ref, gate_ref, gb_ref, ng_ref, o_ref, s_ref, *, n_chunks):
    @pl.when(pl.program_id(0) == 0)
    def _():
        s_ref[...] = jnp.zeros(s_ref.shape, F32)

    C, H, Dh = CHUNK, GDN_HEADS, GDN_HEAD_DIM
    W = H * Dh
    rows_per_chunk = C // PHASES
    nb = q_ref.shape[0]
    ri = lax.broadcasted_iota(jnp.int32, (C, H * C), 0)
    li = lax.broadcasted_iota(jnp.int32, (C, H * C), 1)
    lb, ci = li // C, li % C
    ti = PHASES * (ri % rows_per_chunk) + ri // rows_per_chunk
    tj = PHASES * (ci % rows_per_chunk) + ci // rows_per_chunk
    causal = ti >= tj
    strict = ti > tj
    eye = jnp.where(ri == ci, 1.0, 0.0).astype(F32)
    diag_mask = (ti // 16) == (tj // 16)
    off_masks = [((ti // (2 * s)) == (tj // (2 * s))) & ((ti // s) > (tj // s)) for s in (16, 32)]
    lane_masks = [jnp.where(lb == u, 1.0, 0.0).astype(BF16) for u in range(H)]
    lower1 = jnp.where(causal[:, :C], 1.0, 0.0).astype(F32)
    upper4 = jnp.where(ti <= tj, 1.0, 0.0).astype(F32)
    eye1 = eye[:, :C]
    lb_row = lb[0:1, :]
    hi = lax.Precision.HIGHEST
    ng = ng_ref[...]
    zero_k = jnp.zeros((C, Dh), BF16)

    def load(ref, b, c, h):
        r0 = c * rows_per_chunk
        return jnp.concatenate([ref[b, r0:r0 + rows_per_chunk, p * W + h * Dh:p * W + (h + 1) * Dh]
                                for p in range(PHASES)], axis=0)

    def pick_head(parts):
        out = parts[H - 1]
        for u in range(H - 2, -1, -1):
            out = jnp.where(lb == u, parts[u], out)
        return out

    groups = [(c, b) for c in range(n_chunks) for b in range(nb)]
    A4s, KKs, decays, rhss, qgss, kdecss, elasts = [], [], [], [], [], [], []
    for c, b in groups:
        gbc = gb_ref[b, c]
        gc_rows = jnp.dot(gbc, upper4, precision=hi, preferred_element_type=F32)
        gc_col = lax.dot_general(lower1, gbc, NT_DIMS, precision=hi, preferred_element_type=F32)
        gb_col = lax.dot_general(eye1, gbc, NT_DIMS, precision=hi, preferred_element_type=F32)
        grow4 = gc_rows[H - 1:H, :]
        for u in range(H - 2, -1, -1):
            grow4 = jnp.where(lb_row == u, gc_rows[u:u + 1, :], grow4)
        gcol4 = pick_head([gc_col[:, u:u + 1] for u in range(H)])
        decays.append(jnp.where(causal, jnp.exp(jnp.where(causal, gcol4 - grow4, 0.0)), 0.0))
        kbqs, kks, rhs, qgs, kdecs, els = [], [], [], [], [], []
        for h in range(H):
            qh = load(q_ref, b, c, h)
            kh = load(k_ref, b, c, h)
            vh = load(v_ref, b, c, h)
            gcol = gc_col[:, h:h + 1]
            bcol = gb_col[:, H + h:H + h + 1]
            g_last = gc_rows[h:h + 1, C - 1:C]
            egc = jnp.exp(gcol)
            kb = kh * bcol
            kbqs.append(jnp.concatenate([kb, qh], axis=0).astype(BF16))
            kb16 = kh.astype(BF16)
            kks.append(jnp.concatenate([kb16 if u == h else zero_k for u in range(H)], axis=1))
            rhs.append(jnp.concatenate([vh * bcol, kb * egc], axis=1).astype(BF16))
            qgs.append(qh * egc)
            kdecs.append(kh * jnp.exp(g_last - gcol))
            els.append(jnp.exp(g_last))
        A4s.append(jnp.concatenate(kbqs, axis=1))
        KKs.append(jnp.concatenate(kks, axis=0))
        rhss.append(jnp.concatenate(rhs, axis=0))
        qgss.append(qgs)
        kdecss.append(kdecs)
        elasts.append(els)
    prods = [lax.dot_general(a, kk, NT_DIMS, preferred_element_type=F32) for a, kk in zip(A4s, KKs)]
    Ls = [jnp.where(strict, p[:C] * d, 0.0) for p, d in zip(prods, decays)]
    attns = [_block_diag(p[C:] * d, lane_masks) for p, d in zip(prods, decays)]
    Ts = _packed_lower_inverse(Ls, eye, diag_mask, off_masks, lane_masks)
    sols = [jnp.dot(_block_diag(T, lane_masks), r, preferred_element_type=F32) for T, r in zip(Ts, rhss)]

    for c in range(n_chunks):
        gi = [c * nb + b for b in range(nb)]
        Ss = [[s_ref[b * H + h] for h in range(H)] for b in range(nb)]
        ws_qs = [[_mm(jnp.concatenate([sols[g][h * C:(h + 1) * C, Dh:], qgss[g][h]], axis=0), Ss[b][h])
                  for h in range(H)] for b, g in enumerate(gi)]
        v_new = [[(sols[g][h * C:(h + 1) * C, :Dh] - ws_qs[b][h][:C]).astype(BF16) for h in range(H)]
                 for b, g in enumerate(gi)]
        o_att = [jnp.dot(attns[g], jnp.concatenate(v_new[b], axis=0), preferred_element_type=F32)
                 for b, g in enumerate(gi)]
        for b, g in enumerate(gi):
            for h in range(H):
                s_ref[b * H + h] = Ss[b][h] * elasts[g][h] + _mm_tn(kdecss[g][h], v_new[b][h])
        r0 = c * rows_per_chunk
        for b, g in enumerate(gi):
            for h in range(H):
                o = ws_qs[b][h][C:] + o_att[b][h * C:(h + 1) * C]
                on = _rms(o, ng) * load(gate_ref, b, c, h)
                for p in range(PHASES):
                    o_ref[b, r0:r0 + rows_per_chunk, p * W + h * Dh:p * W + (h + 1) * Dh] = (
                        on[p * rows_per_chunk:(p + 1) * rows_per_chunk])


def _gdn(q, k, v, gate, gb, norm_g, *, tg):
    B, TP, WP = q.shape
    n_chunks = tg // CHUNK
    row_spec = pl.BlockSpec((B, tg // PHASES, WP), lambda i: (0, i, 0))
    return pl.pallas_call(
        functools.partial(_gdn_kernel, n_chunks=n_chunks),
        grid=(TP * PHASES // tg,),
        in_specs=[row_spec, row_spec, row_spec, row_spec,
                  pl.BlockSpec((B, n_chunks, 2 * GDN_HEADS, CHUNK), lambda i: (0, i, 0, 0)),
                  _const_spec(norm_g.shape)],
        out_specs=row_spec,
        out_shape=jax.ShapeDtypeStruct((B, TP, WP), F32),
        scratch_shapes=[pltpu.VMEM((B * GDN_HEADS, GDN_HEAD_DIM, GDN_HEAD_DIM), F32)],
        compiler_params=_params(1),
        name="gdn",
    )(q, k, v, gate, gb, norm_g)


def _attn_kernel(x_ref, og_ref, c_ref, g_ref, wout_ref, kv_ref, wq_ref, wo_ref, h_ref, *, n_split):
    D = x_ref.shape[2] // PHASES
    R = x_ref.shape[1]
    dh = D // XATTN_HEADS
    per = PHASES // n_split
    groups = [range(i * per, (i + 1) * per) for i in range(n_split)]
    hs = range(XATTN_HEADS)

    def rows(ref, grp):
        w = ref.shape[2] // PHASES
        return jnp.concatenate([ref[0, :, v * w:(v + 1) * w] for v in grp], axis=0)

    mixes = [jnp.concatenate([rows(og_ref, grp).astype(BF16), rows(c_ref, grp).astype(BF16)], axis=1)
             for grp in groups]
    ys = [jnp.dot(m, wout_ref[...], preferred_element_type=F32) for m in mixes]
    h1s = [rows(x_ref, grp) + _rms(y, g_ref[1:2, :]) for grp, y in zip(groups, ys)]
    hns = [_rms(h1, g_ref[2:3, :]).astype(BF16) for h1 in h1s]
    qs = [(jnp.dot(hn, wq_ref[...], preferred_element_type=F32) * (dh ** -0.5)).astype(BF16) for hn in hns]
    ss = [[_mm_nt(q[:, hd * dh:(hd + 1) * dh], kv_ref[0, :, hd * dh:(hd + 1) * dh]) for hd in hs] for q in qs]
    es = [[jnp.exp(s - jnp.max(s, axis=-1, keepdims=True)) for s in sg] for sg in ss]
    pvs = [[_mm(e, kv_ref[0, :, D + hd * dh:D + (hd + 1) * dh]) for e, hd in zip(eg, hs)] for eg in es]
    os_ = [jnp.concatenate([o * (1.0 / jnp.sum(e, axis=-1, keepdims=True)) for o, e in zip(pg, eg)], axis=1)
           for pg, eg in zip(pvs, es)]
    y2s = [_mm(o, wo_ref[...]) for o in os_]
    for grp, h1, y2 in zip(groups, h1s, y2s):
        out = h1 + _rms(y2, g_ref[3:4, :])
        for i, v in enumerate(grp):
            h_ref[0, :, v * D:(v + 1) * D] = out[i * R:(i + 1) * R, :]


def _attn(x, o_gdn, c, norm_g, w_out, kv, w_q, w_o, *, tm):
    B, TP, DP = x.shape
    M = kv.shape[1]
    row_spec = lambda a: pl.BlockSpec((1, tm // PHASES, a.shape[2]), lambda b, i: (b, i, 0))
    return pl.pallas_call(
        functools.partial(_attn_kernel, n_split=2),
        grid=(B, TP * PHASES // tm),
        in_specs=[row_spec(x), row_spec(o_gdn), row_spec(c), _const_spec(norm_g.shape),
                  _const_spec(w_out.shape), pl.BlockSpec((1, M, kv.shape[2]), lambda b, i: (b, 0, 0)),
                  _const_spec(w_q.shape), _const_spec(w_o.shape)],
        out_specs=row_spec(x),
        out_shape=jax.ShapeDtypeStruct((B, TP, DP), F32),
        compiler_params=_params(2),
        name="attn",
    )(x, o_gdn, c, norm_g, w_out, kv, w_q, w_o)


def _ffn_kernel(h_ref, g_ref, wup_ref, cw_ref, cb_ref, wdn_ref, o_ref, carry, *, tm, fc, conv_k):
    F = wdn_ref.shape[0]
    D = wdn_ref.shape[1]
    n_chunks = F // fc
    R = tm // PHASES
    first_tail = PHASES - (conv_k - 1)

    @pl.when(pl.program_id(1) == 0)
    def _():
        carry[...] = jnp.zeros(carry.shape, F32)

    h2 = jnp.concatenate(_phase_slabs(h_ref, D), axis=0)
    hn = _rms(h2, g_ref[4:5, :]).astype(BF16)
    sub = lax.broadcasted_iota(jnp.int32, (1, SUBLANES, fc), 1)

    def up(j):
        return [jnp.dot(hn, wup_ref[:, lo:lo + fc], preferred_element_type=F32) for lo in (j * fc, F + j * fc)]

    def conv(u, lo):
        slabs = [u[v * R:(v + 1) * R, :] for v in range(PHASES)]
        tails = {w: carry[w - first_tail, :, lo:lo + fc] for w in range(first_tail, PHASES)}
        for w in range(first_tail, PHASES):
            carry[w - first_tail, :, lo:lo + fc] = slabs[w][R - SUBLANES:, :]
        taps = [cw_ref[k:k + 1, lo:lo + fc] for k in range(conv_k)]
        return jnp.concatenate(_phase_conv(slabs, tails, taps, sub, bias=cb_ref[:, lo:lo + fc]), axis=0)

    acts = []
    nxt = up(0)
    for j in range(n_chunks):
        ua, ub = nxt
        if j + 1 < n_chunks:
            nxt = up(j + 1)
        acts.append((_silu(conv(ua, j * fc)) * conv(ub, F + j * fc)).astype(BF16))
    acc = jnp.dot(jnp.concatenate(acts, axis=1), wdn_ref[...], preferred_element_type=F32)
    out = h2 + _rms(acc, g_ref[5:6, :])
    o_ref[0] = pltpu.einshape("vjc->jvc", out.reshape(PHASES, R, D)).reshape(tm, D)


def _ffn(h, norm_g, w_up, conv_w, conv_b, w_down, *, tm, fc):
    B, TP, DP = h.shape
    T, D = TP * PHASES, DP // PHASES
    F = w_down.shape[0]
    conv_k = conv_w.shape[0]
    assert conv_k - 1 < PHASES and tm % (PHASES * SUBLANES) == 0
    row_spec = pl.BlockSpec((1, tm // PHASES, PHASES * D), lambda b, i: (b, i, 0))
    kernel = functools.partial(_ffn_kernel, tm=tm, fc=fc, conv_k=conv_k)
    return pl.pallas_call(
        kernel,
        grid=(B, T // tm),
        in_specs=[row_spec, _const_spec(norm_g.shape), _const_spec(w_up.shape), _const_spec(conv_w.shape),
                  _const_spec(conv_b.shape), _const_spec(w_down.shape)],
        out_specs=pl.BlockSpec((1, tm, D), lambda b, i: (b, i, 0)),
        out_shape=jax.ShapeDtypeStruct((B, T, D), F32),
        scratch_shapes=[pltpu.VMEM((conv_k - 1, SUBLANES, 2 * F), F32)],
        compiler_params=_params(2),
        name="ffn",
    )(h, norm_g, w_up, conv_w, conv_b, w_down)


def _pick_tile(T, pref):
    t = min(T, pref)
    assert T % t == 0 and t % CHUNK == 0, (T, t)
    return t


def _layer(h, mem, g, w_in, gdn_conv_w, a_log, dt_bias, gdn_norm_g, cfm_dw_w, cfm_dw_b, cfm_ln_g, cfm_ln_b,
           w_out, mem_norm_g, xa_w_q, xa_w_kv, xa_w_o, ffn_w_up, ffn_conv_w, ffn_conv_b, ffn_w_down):
    B, T, D = h.shape
    s2 = 4 * GDN_WIDTH
    s4 = s2 + 2 * GDN_HEADS
    w_qkvg = w_in[:, :s2].astype(BF16)
    w_ab = w_in[:, s2:s4].T.astype(BF16)
    w_cfm = w_in[:, s4:].astype(BF16)
    pad = jnp.zeros((GDN_HEADS,), F32)
    a_log8 = jnp.concatenate([a_log.astype(F32), pad])[:, None]
    dt_bias8 = jnp.concatenate([dt_bias.astype(F32), pad])[:, None]

    tm = _pick_tile(T, 512)
    kv = _mem_kv(mem, mem_norm_g[None, :], xa_w_kv.astype(BF16))
    q, k, v, gate, gb, c, hf = _mixer_in(h, g[0:1], w_qkvg, w_ab, w_cfm, gdn_conv_w, a_log8, dt_bias8,
                                         cfm_dw_w, cfm_dw_b[None, :], cfm_ln_g[None, :], cfm_ln_b[None, :], tm=tm)
    o_gdn = _gdn(q, k, v, gate, gb, gdn_norm_g[None, :], tg=_pick_tile(T, 4 * CHUNK))
    hf = _attn(hf, o_gdn, c, g, w_out.astype(BF16), kv, xa_w_q.astype(BF16), xa_w_o.astype(BF16), tm=tm)
    return _ffn(hf, g, ffn_w_up.astype(BF16), ffn_conv_w, ffn_conv_b[None, :], ffn_w_down.astype(BF16),
                tm=tm, fc=256)


def kernel(x, mem, norm_g, w_in, gdn_conv_w, gdn_a_log, gdn_dt_bias, gdn_norm_g, cfm_dw_w, cfm_dw_b,
           cfm_ln_g, cfm_ln_b, w_out, mem_norm_g, xa_w_q, xa_w_kv, xa_w_o, ffn_w_up, ffn_conv_w,
           ffn_conv_b, ffn_w_down):
    h = x
    for l in range(norm_g.shape[0]):
        h = _layer(h, mem, norm_g[l], w_in[l], gdn_conv_w[l], gdn_a_log[l], gdn_dt_bias[l], gdn_norm_g[l],
                   cfm_dw_w[l], cfm_dw_b[l], cfm_ln_g[l], cfm_ln_b[l], w_out[l], mem_norm_g[l],
                   xa_w_q[l], xa_w_kv[l], xa_w_o[l], ffn_w_up[l], ffn_conv_w[l], ffn_conv_b[l], ffn_w_down[l])
    return h
```

```python
import functools

import jax
import jax.numpy as jnp
from jax import lax
from jax.experimental import pallas as pl
from jax.experimental.pallas import tpu as pltpu

EPS = 1e-6
GDN_HEADS = 4
GDN_HEAD_DIM = 128
GDN_WIDTH = GDN_HEADS * GDN_HEAD_DIM
GDN_SHORT_CONV = 4
CHUNK = 64
XATTN_HEADS = 4

F32 = jnp.float32
BF16 = jnp.bfloat16

SUBLANES = 8
PHASES = SUBLANES
VMEM_LIMIT_BYTES = 56 * 1024 * 1024

NT_DIMS = (((1,), (1,)), ((), ()))
TN_DIMS = (((0,), (0,)), ((), ()))


def _mm(a, b):
    return jnp.dot(a.astype(BF16), b.astype(BF16), preferred_element_type=F32)


def _mm_nt(a, b):
    return lax.dot_general(a.astype(BF16), b.astype(BF16), NT_DIMS, preferred_element_type=F32)


def _mm_tn(a, b):
    return lax.dot_general(a.astype(BF16), b.astype(BF16), TN_DIMS, preferred_element_type=F32)


def _split3(x):
    hi = x.astype(BF16)
    r1 = x - hi.astype(F32)
    mid = r1.astype(BF16)
    lo = (r1 - mid.astype(F32)).astype(BF16)
    return hi, mid, lo


def _dot01(x, m01, dims=None):
    out = None
    for part in _split3(x):
        if dims is None:
            d = jnp.dot(part, m01, preferred_element_type=F32)
        else:
            d = lax.dot_general(m01, part, dims, preferred_element_type=F32)
        out = d if out is None else out + d
    return out


def _rms(x, g):
    return x * lax.rsqrt(jnp.mean(x * x, axis=-1, keepdims=True) + EPS) * g


NEG_LOG2_E = -1.4426950408889634


def _sigmoid(x):
    return 1.0 / (1.0 + jnp.exp2(x * NEG_LOG2_E))


def _silu(x):
    return x * _sigmoid(x)


def _delay_rows(x, tail, s, sub):
    rows, width = x.shape
    groups = jnp.concatenate([tail, x], axis=0).reshape(rows // SUBLANES + 1, SUBLANES, width)
    rot = pltpu.roll(groups, s, 1)
    return jnp.where(sub < s, rot[:-1], rot[1:]).reshape(rows, width)


def _phase_slabs(ref, width):
    return [ref[0, :, v * width:(v + 1) * width] for v in range(PHASES)]


def _phase_conv(slabs, tails, taps, sub, bias=None):
    K = len(taps)
    delayed = {w: _delay_rows(slabs[w], tails[w], 1, sub) for w in range(PHASES - (K - 1), PHASES)}
    outs = []
    for v in range(PHASES):
        acc = taps[K - 1] * slabs[v]
        if bias is not None:
            acc = acc + bias
        for d in range(1, K):
            acc = acc + taps[K - 1 - d] * (slabs[v - d] if v >= d else delayed[v - d + PHASES])
        outs.append(acc)
    return outs


def _const_spec(shape):
    zeros = (0,) * len(shape)
    return pl.BlockSpec(shape, lambda *_: zeros)


def _params(n_axes):
    return pltpu.CompilerParams(dimension_semantics=("arbitrary",) * n_axes,
                                vmem_limit_bytes=VMEM_LIMIT_BYTES)


def _mem_kv_kernel(mem_ref, g_ref, w_ref, kv_ref):
    mn = _rms(mem_ref[0], g_ref[...])
    kv_ref[0] = _mm(mn, w_ref[...]).astype(BF16)


def _mem_kv(mem, g, w_kv):
    B, M, D = mem.shape
    N = w_kv.shape[1]
    return pl.pallas_call(
        _mem_kv_kernel,
        grid=(B,),
        in_specs=[pl.BlockSpec((1, M, D), lambda b: (b, 0, 0)), _const_spec((1, D)), _const_spec((D, N))],
        out_specs=pl.BlockSpec((1, M, N), lambda b: (b, 0, 0)),
        out_shape=jax.ShapeDtypeStruct((B, M, N), BF16),
        compiler_params=_params(1),
        name="mem_kv",
    )(mem, g, w_kv)


def _mixer_in_kernel(x_ref, g_ref, wqkvg_ref, wab_ref, wcfm_ref, convw_ref, alog_ref, dtb_ref,
                     dww_ref, dwb_ref, lng_ref, lnb_ref,
                     q_ref, k_ref, v_ref, gate_ref, gb_ref, c_ref, xf_ref,
                     ztail, ctail, cdel, perm, *, tm, cfm_k):
    qkv_w = 3 * GDN_WIDTH
    W = GDN_WIDTH
    D = x_ref.shape[2]
    cw = c_ref.shape[2] // PHASES
    n_split = cdel.shape[0]
    R = tm // PHASES
    Rg = R // n_split
    tg = tm // n_split
    kc = GDN_SHORT_CONV
    first_tail = PHASES - (kc - 1)
    n_del = cdel.shape[1]
    q_scale = GDN_HEAD_DIM ** -0.5

    @pl.when(pl.program_id(1) == 0)
    def _():
        ztail[...] = jnp.zeros(ztail.shape, F32)
        ctail[...] = jnp.zeros(ctail.shape, F32)

    @pl.when((pl.program_id(0) == 0) & (pl.program_id(1) == 0))
    def _():
        src = lax.broadcasted_iota(jnp.int32, (tg, tg), 0)
        dst = lax.broadcasted_iota(jnp.int32, (tg, tg), 1)
        g_, v_, s_ = dst // CHUNK, (dst % CHUNK) // SUBLANES, dst % SUBLANES
        perm[...] = jnp.where(src == v_ * Rg + g_ * SUBLANES + s_, 1.0, 0.0).astype(BF16)

    h3 = pltpu.einshape("jvc->vjc", x_ref[0].reshape(R, PHASES, D))
    for v in range(PHASES):
        xf_ref[0, :, v * D:(v + 1) * D] = h3[v]
    groups = range(n_split)
    rows = [slice(i * Rg, (i + 1) * Rg) for i in groups]
    hbs = [_rms(h3[:, rows[i], :].reshape(tg, D), g_ref[...]).astype(BF16) for i in groups]

    czs = [jnp.dot(hb, wcfm_ref[...], preferred_element_type=F32) for hb in hbs]
    abs_ = [lax.dot_general(wab_ref[...], hb, NT_DIMS, preferred_element_type=F32) for hb in hbs]
    zs_ = [jnp.dot(hb, wqkvg_ref[...], preferred_element_type=F32) for hb in hbs]

    sub_c = lax.broadcasted_iota(jnp.int32, (1, SUBLANES, cw), 1)
    ctails = [ctail[w] for w in range(PHASES)]
    for i in groups:
        for w in range(PHASES):
            glu = czs[i][w * Rg:(w + 1) * Rg, :cw] * _sigmoid(czs[i][w * Rg:(w + 1) * Rg, cw:])
            cdel[i, 0, w * Rg:(w + 1) * Rg, :] = glu
            for m in range(1, n_del):
                cdel[i, m, w * Rg:(w + 1) * Rg, :] = _delay_rows(glu, ctails[w], m, sub_c)
            ctails[w] = glu[Rg - SUBLANES:, :]
    for w in range(PHASES):
        ctail[w] = ctails[w]
    for i in groups:
        for v in range(PHASES):
            acc = dwb_ref[...]
            for d in range(cfm_k):
                a, e = divmod(d, PHASES)
                w, m = (v - e) % PHASES, a + (1 if v < e else 0)
                acc = acc + dww_ref[cfm_k - 1 - d:cfm_k - d, :] * cdel[i, m, w * Rg:(w + 1) * Rg, :]
            mu = jnp.mean(acc, axis=-1, keepdims=True)
            xc = acc - mu
            var = jnp.mean(xc * xc, axis=-1, keepdims=True)
            c_ref[0, rows[i], v * cw:(v + 1) * cw] = _silu(xc * lax.rsqrt(var + EPS) * lng_ref[...] + lnb_ref[...])

    for i in groups:
        ab = _dot01(abs_[i], perm[...])
        sp_in = ab + dtb_ref[...]
        softplus = jnp.maximum(sp_in, 0.0) + jnp.log(1.0 + jnp.exp(-jnp.abs(sp_in)))
        g = -jnp.exp(alog_ref[...]) * softplus
        row = lax.broadcasted_iota(jnp.int32, ab.shape, 0)
        gb = jnp.where(row < GDN_HEADS, g, _sigmoid(ab))
        for c in range(tg // CHUNK):
            gb_ref[0, i * (tg // CHUNK) + c] = gb[:, c * CHUNK:(c + 1) * CHUNK]

    sub_z = lax.broadcasted_iota(jnp.int32, (1, SUBLANES, qkv_w), 1)
    taps = [convw_ref[j:j + 1, :] for j in range(kc)]
    tails = {w: ztail[w - first_tail] for w in range(first_tail, PHASES)}
    for i in groups:
        z = zs_[i]
        zs = [z[v * Rg:(v + 1) * Rg, :qkv_w] for v in range(PHASES)]
        for v, acc in enumerate(_phase_conv(zs, tails, taps, sub_z)):
            qkv = _silu(acc)
            gate_ref[0, rows[i], v * W:(v + 1) * W] = _silu(z[v * Rg:(v + 1) * Rg, qkv_w:])
            for hd in range(GDN_HEADS):
                lo = hd * GDN_HEAD_DIM
                qh = qkv[:, lo:lo + GDN_HEAD_DIM]
                kh = qkv[:, W + lo:W + lo + GDN_HEAD_DIM]
                q_ref[0, rows[i], v * W + lo:v * W + lo + GDN_HEAD_DIM] = qh * (
                    lax.rsqrt(jnp.sum(qh * qh, axis=-1, keepdims=True) + EPS) * q_scale)
                k_ref[0, rows[i], v * W + lo:v * W + lo + GDN_HEAD_DIM] = kh * lax.rsqrt(
                    jnp.sum(kh * kh, axis=-1, keepdims=True) + EPS)
            v_ref[0, rows[i], v * W:(v + 1) * W] = qkv[:, 2 * W:]
        tails = {w: zs[w][Rg - SUBLANES:, :] for w in range(first_tail, PHASES)}
    for w in range(first_tail, PHASES):
        ztail[w - first_tail] = tails[w]


def _mixer_in(x, g0, w_qkvg, w_ab, w_cfm, conv_w, a_log8, dt_bias8, dw_w, dw_b, ln_g, ln_b, *, tm):
    B, T, D = x.shape
    cfm_k, cw = dw_w.shape
    n_split = 2
    assert GDN_SHORT_CONV - 1 < PHASES and tm % (n_split * CHUNK) == 0
    n_del = (cfm_k - 1) // PHASES + 2
    assert n_del - 1 < SUBLANES
    R = tm // PHASES
    row_spec = lambda w: pl.BlockSpec((1, R, PHASES * w), lambda b, i: (b, i, 0))
    kernel = functools.partial(_mixer_in_kernel, tm=tm, cfm_k=cfm_k)
    wide = jax.ShapeDtypeStruct((B, T // PHASES, PHASES * GDN_WIDTH), F32)
    return pl.pallas_call(
        kernel,
        grid=(B, T // tm),
        in_specs=[pl.BlockSpec((1, tm, D), lambda b, i: (b, i, 0)), _const_spec(g0.shape), _const_spec(w_qkvg.shape), _const_spec(w_ab.shape),
                  _const_spec(w_cfm.shape), _const_spec(conv_w.shape), _const_spec(a_log8.shape),
                  _const_spec(dt_bias8.shape), _const_spec(dw_w.shape), _const_spec(dw_b.shape),
                  _const_spec(ln_g.shape), _const_spec(ln_b.shape)],
        out_specs=[row_spec(GDN_WIDTH), row_spec(GDN_WIDTH), row_spec(GDN_WIDTH), row_spec(GDN_WIDTH),
                   pl.BlockSpec((1, tm // CHUNK, 2 * GDN_HEADS, CHUNK), lambda b, i: (b, i, 0, 0)),
                   row_spec(cw), row_spec(D)],
        out_shape=[wide, wide, wide, wide,
                   jax.ShapeDtypeStruct((B, T // CHUNK, 2 * GDN_HEADS, CHUNK), F32),
                   jax.ShapeDtypeStruct((B, T // PHASES, PHASES * cw), F32),
                   jax.ShapeDtypeStruct((B, T // PHASES, PHASES * D), F32)],
        scratch_shapes=[pltpu.VMEM((GDN_SHORT_CONV - 1, SUBLANES, 3 * GDN_WIDTH), F32),
                        pltpu.VMEM((PHASES, SUBLANES, cw), F32),
                        pltpu.VMEM((n_split, n_del, tm // n_split, cw), F32),
                        pltpu.VMEM((tm // n_split, tm // n_split), BF16)],
        compiler_params=_params(2),
        name="mixer_in",
    )(x, g0, w_qkvg, w_ab, w_cfm, conv_w, a_log8, dt_bias8, dw_w, dw_b, ln_g, ln_b)


def _block_diag(x4, lane_masks):
    xb = x4.astype(BF16)
    return jnp.concatenate([xb * m for m in lane_masks], axis=0)


def _packed_lower_inverse(Ls, eye, diag_mask, off_masks, lane_masks):
    mmp = lambda a, bd: jnp.dot(a.astype(BF16), bd, preferred_element_type=F32)
    Ps = [jnp.where(diag_mask, L, 0.0) for L in Ls]
    Ts = [eye - P for P in Ps]
    for _ in range(3):
        Ps = [mmp(P, _block_diag(P, lane_masks)) for P in Ps]
        Bs = [_block_diag(P, lane_masks) for P in Ps]
        Ts = [T + mmp(T, B) for T, B in zip(Ts, Bs)]
    for m in off_masks:
        Xs = [mmp(T, _block_diag(jnp.where(m, L, 0.0), lane_masks)) for T, L in zip(Ts, Ls)]
        Ts = [T - mmp(X, _block_diag(T, lane_masks)) for T, X in zip(Ts, Xs)]
    return Ts


def _gdn_kernel(q_ref, k_ref, v_ref, gate_ref, gb_ref, ng_ref, o_ref, s_ref, *, n_chunks):
    @pl.when(pl.program_id(0) == 0)
    def _():
        s_ref[...] = jnp.zeros(s_ref.shape, F32)

    C, H, Dh = CHUNK, GDN_HEADS, GDN_HEAD_DIM
    W = H * Dh
    rows_per_chunk = C // PHASES
    nb = q_ref.shape[0]
    ri = lax.broadcasted_iota(jnp.int32, (C, H * C), 0)
    li = lax.broadcasted_iota(jnp.int32, (C, H * C), 1)
    lb, ci = li // C, li % C
    ti = PHASES * (ri % rows_per_chunk) + ri // rows_per_chunk
    tj = PHASES * (ci % rows_per_chunk) + ci // rows_per_chunk
    causal = ti >= tj
    strict = ti > tj
    eye = jnp.where(ri == ci, 1.0, 0.0).astype(F32)
    diag_mask = (ti // 16) == (tj // 16)
    off_masks = [((ti // (2 * s)) == (tj // (2 * s))) & ((ti // s) > (tj // s)) for s in (16, 32)]
    lane_masks = [jnp.where(lb == u, 1.0, 0.0).astype(BF16) for u in range(H)]
    lower1 = jnp.where(causal[:, :C], 1.0, 0.0).astype(BF16)
    upper4 = jnp.where(ti <= tj, 1.0, 0.0).astype(BF16)
    eye1 = eye[:, :C].astype(BF16)
    lb_row = lb[0:1, :]
    ng = ng_ref[...]
    zero_k = jnp.zeros((C, Dh), BF16)

    def load(ref, b, c, h):
        r0 = c * rows_per_chunk
        return jnp.concatenate([ref[b, r0:r0 + rows_per_chunk, p * W + h * Dh:p * W + (h + 1) * Dh]
                                for p in range(PHASES)], axis=0)

    def pick_head(parts):
        out = parts[H - 1]
        for u in range(H - 2, -1, -1):
            out = jnp.where(lb == u, parts[u], out)
        return out

    groups = [(c, b) for c in range(n_chunks) for b in range(nb)]
    A4s, KKs, decays, rhss, qgss, kdecss, elasts = [], [], [], [], [], [], []
    for c, b in groups:
        gbc = gb_ref[b, c]
        gc_rows = _dot01(gbc, upper4)
        gc_col = _dot01(gbc, lower1, NT_DIMS)
        gb_col = _dot01(gbc, eye1, NT_DIMS)
        grow4 = gc_rows[H - 1:H, :]
        for u in range(H - 2, -1, -1):
            grow4 = jnp.where(lb_row == u, gc_rows[u:u + 1, :], grow4)
        gcol4 = pick_head([gc_col[:, u:u + 1] for u in range(H)])
        decays.append(jnp.where(causal, jnp.exp(jnp.where(causal, gcol4 - grow4, 0.0)), 0.0))
        kbqs, kks, rhs, qgs, kdecs, els = [], [], [], [], [], []
        for h in range(H):
            qh = load(q_ref, b, c, h)
            kh = load(k_ref, b, c, h)
            vh = load(v_ref, b, c, h)
            gcol = gc_col[:, h:h + 1]
            bcol = gb_col[:, H + h:H + h + 1]
            g_last = gc_rows[h:h + 1, C - 1:C]
            egc = jnp.exp(gcol)
            kb = kh * bcol
            kbqs.append(jnp.concatenate([kb, qh], axis=0).astype(BF16))
            kb16 = kh.astype(BF16)
            kks.append(jnp.concatenate([kb16 if u == h else zero_k for u in range(H)], axis=1))
            rhs.append(jnp.concatenate([vh * bcol, kb * egc], axis=1).astype(BF16))
            qgs.append(qh * egc)
            kdecs.append(kh * jnp.exp(g_last - gcol))
            els.append(jnp.exp(g_last))
        A4s.append(jnp.concatenate(kbqs, axis=1))
        KKs.append(jnp.concatenate(kks, axis=0))
        rhss.append(jnp.concatenate(rhs, axis=0))
        qgss.append(qgs)
        kdecss.append(kdecs)
        elasts.append(els)
    prods = [lax.dot_general(a, kk, NT_DIMS, preferred_element_type=F32) for a, kk in zip(A4s, KKs)]
    Ls = [jnp.where(strict, p[:C] * d, 0.0) for p, d in zip(prods, decays)]
    attns = [_block_diag(p[C:] * d, lane_masks) for p, d in zip(prods, decays)]
    Ts = _packed_lower_inverse(Ls, eye, diag_mask, off_masks, lane_masks)
    sols = [jnp.dot(_block_diag(T, lane_masks), r, preferred_element_type=F32) for T, r in zip(Ts, rhss)]

    for c in range(n_chunks):
        gi = [c * nb + b for b in range(nb)]
        Ss = [[s_ref[b * H + h] for h in range(H)] for b in range(nb)]
        ws_qs = [[_mm(jnp.concatenate([sols[g][h * C:(h + 1) * C, Dh:], qgss[g][h]], axis=0), Ss[b][h])
                  for h in range(H)] for b, g in enumerate(gi)]
        v_new = [[(sols[g][h * C:(h + 1) * C, :Dh] - ws_qs[b][h][:C]).astype(BF16) for h in range(H)]
                 for b, g in enumerate(gi)]
        o_att = [jnp.dot(attns[g], jnp.concatenate(v_new[b], axis=0), preferred_element_type=F32)
                 for b, g in enumerate(gi)]
        for b, g in enumerate(gi):
            for h in range(H):
                s_ref[b * H + h] = Ss[b][h] * elasts[g][h] + _mm_tn(kdecss[g][h], v_new[b][h])
        r0 = c * rows_per_chunk
        for b, g in enumerate(gi):
            for h in range(H):
                o = ws_qs[b][h][C:] + o_att[b][h * C:(h + 1) * C]
                on = _rms(o, ng) * load(gate_ref, b, c, h)
                for p in range(PHASES):
                    o_ref[b, r0:r0 + rows_per_chunk, p * W + h * Dh:p * W + (h + 1) * Dh] = (
                        on[p * rows_per_chunk:(p + 1) * rows_per_chunk])


def _gdn(q, k, v, gate, gb, norm_g, *, tg):
    B, TP, WP = q.shape
    n_chunks = tg // CHUNK
    row_spec = pl.BlockSpec((B, tg // PHASES, WP), lambda i: (0, i, 0))
    return pl.pallas_call(
        functools.partial(_gdn_kernel, n_chunks=n_chunks),
        grid=(TP * PHASES // tg,),
        in_specs=[row_spec, row_spec, row_spec, row_spec,
                  pl.BlockSpec((B, n_chunks, 2 * GDN_HEADS, CHUNK), lambda i: (0, i, 0, 0)),
                  _const_spec(norm_g.shape)],
        out_specs=row_spec,
        out_shape=jax.ShapeDtypeStruct((B, TP, WP), F32),
        scratch_shapes=[pltpu.VMEM((B * GDN_HEADS, GDN_HEAD_DIM, GDN_HEAD_DIM), F32)],
        compiler_params=_params(1),
        name="gdn",
    )(q, k, v, gate, gb, norm_g)


def _attn_kernel(x_ref, og_ref, c_ref, g_ref, wout_ref, kv_ref, wq_ref, wo_ref, h_ref, *, n_split):
    D = x_ref.shape[2] // PHASES
    R = x_ref.shape[1]
    dh = D // XATTN_HEADS
    per = PHASES // n_split
    groups = [range(i * per, (i + 1) * per) for i in range(n_split)]
    hs = range(XATTN_HEADS)

    def rows(ref, grp):
        w = ref.shape[2] // PHASES
        return jnp.concatenate([ref[0, :, v * w:(v + 1) * w] for v in grp], axis=0)

    mixes = [jnp.concatenate([rows(og_ref, grp).astype(BF16), rows(c_ref, grp).astype(BF16)], axis=1)
             for grp in groups]
    ys = [jnp.dot(m, wout_ref[...], preferred_element_type=F32) for m in mixes]
    h1s = [rows(x_ref, grp) + _rms(y, g_ref[1:2, :]) for grp, y in zip(groups, ys)]
    hns = [_rms(h1, g_ref[2:3, :]).astype(BF16) for h1 in h1s]
    qs = [(jnp.dot(hn, wq_ref[...], preferred_element_type=F32) * (dh ** -0.5)).astype(BF16) for hn in hns]
    ss = [[_mm_nt(q[:, hd * dh:(hd + 1) * dh], kv_ref[0, :, hd * dh:(hd + 1) * dh]) for hd in hs] for q in qs]
    es = [[jnp.exp(s - jnp.max(s, axis=-1, keepdims=True)) for s in sg] for sg in ss]
    pvs = [[_mm(e, kv_ref[0, :, D + hd * dh:D + (hd + 1) * dh]) for e, hd in zip(eg, hs)] for eg in es]
    os_ = [jnp.concatenate([o * (1.0 / jnp.sum(e, axis=-1, keepdims=True)) for o, e in zip(pg, eg)], axis=1)
           for pg, eg in zip(pvs, es)]
    y2s = [_mm(o, wo_ref[...]) for o in os_]
    for grp, h1, y2 in zip(groups, h1s, y2s):
        out = h1 + _rms(y2, g_ref[3:4, :])
        for i, v in enumerate(grp):
            h_ref[0, :, v * D:(v + 1) * D] = out[i * R:(i + 1) * R, :]


def _attn(x, o_gdn, c, norm_g, w_out, kv, w_q, w_o, *, tm):
    B, TP, DP = x.shape
    M = kv.shape[1]
    row_spec = lambda a: pl.BlockSpec((1, tm // PHASES, a.shape[2]), lambda b, i: (b, i, 0))
    return pl.pallas_call(
        functools.partial(_attn_kernel, n_split=2),
        grid=(B, TP * PHASES // tm),
        in_specs=[row_spec(x), row_spec(o_gdn), row_spec(c), _const_spec(norm_g.shape),
                  _const_spec(w_out.shape), pl.BlockSpec((1, M, kv.shape[2]), lambda b, i: (b, 0, 0)),
                  _const_spec(w_q.shape), _const_spec(w_o.shape)],
        out_specs=row_spec(x),
        out_shape=jax.ShapeDtypeStruct((B, TP, DP), F32),
        compiler_params=_params(2),
        name="attn",
    )(x, o_gdn, c, norm_g, w_out, kv, w_q, w_o)


def _ffn_kernel(h_ref, g_ref, wup_ref, cw_ref, cb_ref, wdn_ref, o_ref, carry, *, tm, fc, conv_k):
    F = wdn_ref.shape[0]
    D = wdn_ref.shape[1]
    n_chunks = F // fc
    R = tm // PHASES
    first_tail = PHASES - (conv_k - 1)

    @pl.when(pl.program_id(1) == 0)
    def _():
        carry[...] = jnp.zeros(carry.shape, F32)

    h2 = jnp.concatenate(_phase_slabs(h_ref, D), axis=0)
    hn = _rms(h2, g_ref[4:5, :]).astype(BF16)
    sub = lax.broadcasted_iota(jnp.int32, (1, SUBLANES, fc), 1)

    def up(j):
        return [jnp.dot(hn, wup_ref[:, lo:lo + fc], preferred_element_type=F32) for lo in (j * fc, F + j * fc)]

    def conv(u, lo):
        slabs = [u[v * R:(v + 1) * R, :] for v in range(PHASES)]
        tails = {w: carry[w - first_tail, :, lo:lo + fc] for w in range(first_tail, PHASES)}
        for w in range(first_tail, PHASES):
            carry[w - first_tail, :, lo:lo + fc] = slabs[w][R - SUBLANES:, :]
        taps = [cw_ref[k:k + 1, lo:lo + fc] for k in range(conv_k)]
        return jnp.concatenate(_phase_conv(slabs, tails, taps, sub, bias=cb_ref[:, lo:lo + fc]), axis=0)

    acts = []
    nxt = up(0)
    for j in range(n_chunks):
        ua, ub = nxt
        if j + 1 < n_chunks:
            nxt = up(j + 1)
        acts.append((_silu(conv(ua, j * fc)) * conv(ub, F + j * fc)).astype(BF16))
    acc = jnp.dot(jnp.concatenate(acts, axis=1), wdn_ref[...], preferred_element_type=F32)
    out = h2 + _rms(acc, g_ref[5:6, :])
    o_ref[0] = pltpu.einshape("vjc->jvc", out.reshape(PHASES, R, D)).reshape(tm, D)


def _ffn(h, norm_g, w_up, conv_w, conv_b, w_down, *, tm, fc):
    B, TP, DP = h.shape
    T, D = TP * PHASES, DP // PHASES
    F = w_down.shape[0]
    conv_k = conv_w.shape[0]
    assert conv_k - 1 < PHASES and tm % (PHASES * SUBLANES) == 0
    row_spec = pl.BlockSpec((1, tm // PHASES, PHASES * D), lambda b, i: (b, i, 0))
    kernel = functools.partial(_ffn_kernel, tm=tm, fc=fc, conv_k=conv_k)
    return pl.pallas_call(
        kernel,
        grid=(B, T // tm),
        in_specs=[row_spec, _const_spec(norm_g.shape), _const_spec(w_up.shape), _const_spec(conv_w.shape),
                  _const_spec(conv_b.shape), _const_spec(w_down.shape)],
        out_specs=pl.BlockSpec((1, tm, D), lambda b, i: (b, i, 0)),
        out_shape=jax.ShapeDtypeStruct((B, T, D), F32),
        scratch_shapes=[pltpu.VMEM((conv_k - 1, SUBLANES, 2 * F), F32)],
        compiler_params=_params(2),
        name="ffn",
    )(h, norm_g, w_up, conv_w, conv_b, w_down)


def _pick_tile(T, pref):
    t = min(T, pref)
    assert T % t == 0 and t % CHUNK == 0, (T, t)
    return t


def _layer(h, mem, g, w_in, gdn_conv_w, a_log, dt_bias, gdn_norm_g, cfm_dw_w, cfm_dw_b, cfm_ln_g, cfm_ln_b,
           w_out, mem_norm_g, xa_w_q, xa_w_kv, xa_w_o, ffn_w_up, ffn_conv_w, ffn_conv_b, ffn_w_down):
    B, T, D = h.shape
    s2 = 4 * GDN_WIDTH
    s4 = s2 + 2 * GDN_HEADS
    w_qkvg = w_in[:, :s2].astype(BF16)
    w_ab = w_in[:, s2:s4].T.astype(BF16)
    w_cfm = w_in[:, s4:].astype(BF16)
    pad = jnp.zeros((GDN_HEADS,), F32)
    a_log8 = jnp.concatenate([a_log.astype(F32), pad])[:, None]
    dt_bias8 = jnp.concatenate([dt_bias.astype(F32), pad])[:, None]

    tm = _pick_tile(T, 512)
    kv = _mem_kv(mem, mem_norm_g[None, :], xa_w_kv.astype(BF16))
    q, k, v, gate, gb, c, hf = _mixer_in(h, g[0:1], w_qkvg, w_ab, w_cfm, gdn_conv_w, a_log8, dt_bias8,
                                         cfm_dw_w, cfm_dw_b[None, :], cfm_ln_g[None, :], cfm_ln_b[None, :], tm=tm)
    o_gdn = _gdn(q, k, v, gate, gb, gdn_norm_g[None, :], tg=_pick_tile(T, 4 * CHUNK))
    hf = _attn(hf, o_gdn, c, g, w_out.astype(BF16), kv, xa_w_q.astype(BF16), xa_w_o.astype(BF16), tm=tm)
    return _ffn(hf, g, ffn_w_up.astype(BF16), ffn_conv_w, ffn_conv_b[None, :], ffn_w_down.astype(BF16),
                tm=tm, fc=256)


def kernel(x, mem, norm_g, w_in, gdn_conv_w, gdn_a_log, gdn_dt_bias, gdn_norm_g, cfm_dw_w, cfm_dw_b,
           cfm_ln_g, cfm_ln_b, w_out, mem_norm_g, xa_w_q, xa_w_kv, xa_w_o, ffn_w_up, ffn_conv_w,
           ffn_conv_b, ffn_w_down):
    h = x
    for l in range(norm_g.shape[0]):
        h = _layer(h, mem, norm_g[l], w_in[l], gdn_conv_w[l], gdn_a_log[l], gdn_dt_bias[l], gdn_norm_g[l],
                   cfm_dw_w[l], cfm_dw_b[l], cfm_ln_g[l], cfm_ln_b[l], w_out[l], mem_norm_g[l],
                   xa_w_q[l], xa_w_kv[l], xa_w_o[l], ffn_w_up[l], ffn_conv_w[l], ffn_conv_b[l], ffn_w_down[l])
    return h
```

```python
import functools

import jax
import jax.numpy as jnp
from jax import lax
from jax.experimental import pallas as pl
from jax.experimental.pallas import tpu as pltpu

EPS = 1e-6
GDN_HEADS = 4
GDN_HEAD_DIM = 128
GDN_WIDTH = GDN_HEADS * GDN_HEAD_DIM
GDN_SHORT_CONV = 4
CHUNK = 64
XATTN_HEADS = 4

F32 = jnp.float32
BF16 = jnp.bfloat16

SUBLANES = 8
PHASES = SUBLANES
VMEM_LIMIT_BYTES = 56 * 1024 * 1024

NT_DIMS = (((1,), (1,)), ((), ()))
TN_DIMS = (((0,), (0,)), ((), ()))


def _mm(a, b):
    return jnp.dot(a.astype(BF16), b.astype(BF16), preferred_element_type=F32)


def _mm_nt(a, b):
    return lax.dot_general(a.astype(BF16), b.astype(BF16), NT_DIMS, preferred_element_type=F32)


def _mm_tn(a, b):
    return lax.dot_general(a.astype(BF16), b.astype(BF16), TN_DIMS, preferred_element_type=F32)


def _split3(x):
    hi = x.astype(BF16)
    r1 = x - hi.astype(F32)
    mid = r1.astype(BF16)
    lo = (r1 - mid.astype(F32)).astype(BF16)
    return hi, mid, lo


def _dot01(x, m01, dims=None):
    out = None
    for part in _split3(x):
        if dims is None:
            d = jnp.dot(part, m01, preferred_element_type=F32)
        else:
            d = lax.dot_general(m01, part, dims, preferred_element_type=F32)
        out = d if out is None else out + d
    return out


def _rms(x, g):
    return x * lax.rsqrt(jnp.mean(x * x, axis=-1, keepdims=True) + EPS) * g


NEG_LOG2_E = -1.4426950408889634


def _sigmoid(x):
    return 1.0 / (1.0 + jnp.exp2(x * NEG_LOG2_E))


def _silu(x):
    return x * _sigmoid(x)


def _delay_rows(x, tail, s, sub):
    rows, width = x.shape
    groups = jnp.concatenate([tail, x], axis=0).reshape(rows // SUBLANES + 1, SUBLANES, width)
    rot = pltpu.roll(groups, s, 1)
    return jnp.where(sub < s, rot[:-1], rot[1:]).reshape(rows, width)


def _phase_slabs(ref, width):
    return [ref[0, :, v * width:(v + 1) * width] for v in range(PHASES)]


def _phase_conv(slabs, tails, taps, sub, bias=None):
    K = len(taps)
    delayed = {w: _delay_rows(slabs[w], tails[w], 1, sub) for w in range(PHASES - (K - 1), PHASES)}
    outs = []
    for v in range(PHASES):
        acc = taps[K - 1] * slabs[v]
        if bias is not None:
            acc = acc + bias
        for d in range(1, K):
            acc = acc + taps[K - 1 - d] * (slabs[v - d] if v >= d else delayed[v - d + PHASES])
        outs.append(acc)
    return outs


def _const_spec(shape):
    zeros = (0,) * len(shape)
    return pl.BlockSpec(shape, lambda *_: zeros)


def _params(n_axes):
    return pltpu.CompilerParams(dimension_semantics=("arbitrary",) * n_axes,
                                vmem_limit_bytes=VMEM_LIMIT_BYTES)


def _mem_kv_kernel(mem_ref, g_ref, w_ref, kv_ref):
    mn = _rms(mem_ref[0], g_ref[...])
    kv_ref[0] = _mm(mn, w_ref[...]).astype(BF16)


def _mem_kv(mem, g, w_kv):
    B, M, D = mem.shape
    N = w_kv.shape[1]
    return pl.pallas_call(
        _mem_kv_kernel,
        grid=(B,),
        in_specs=[pl.BlockSpec((1, M, D), lambda b: (b, 0, 0)), _const_spec((1, D)), _const_spec((D, N))],
        out_specs=pl.BlockSpec((1, M, N), lambda b: (b, 0, 0)),
        out_shape=jax.ShapeDtypeStruct((B, M, N), BF16),
        compiler_params=_params(1),
        name="mem_kv",
    )(mem, g, w_kv)


def _mixer_in_kernel(x_ref, g_ref, wqkvg_ref, wab_ref, wcfm_ref, convw_ref, alog_ref, dtb_ref,
                     dww_ref, dwb_ref, lng_ref, lnb_ref,
                     q_ref, k_ref, v_ref, gate_ref, gb_ref, c_ref, xf_ref,
                     ztail, ctail, cdel, perm, *, tm, cfm_k):
    qkv_w = 3 * GDN_WIDTH
    W = GDN_WIDTH
    D = x_ref.shape[2]
    cw = c_ref.shape[2] // PHASES
    n_split = cdel.shape[0]
    R = tm // PHASES
    Rg = R // n_split
    tg = tm // n_split
    kc = GDN_SHORT_CONV
    first_tail = PHASES - (kc - 1)
    n_del = cdel.shape[1]
    q_scale = GDN_HEAD_DIM ** -0.5

    @pl.when(pl.program_id(1) == 0)
    def _():
        ztail[...] = jnp.zeros(ztail.shape, F32)
        ctail[...] = jnp.zeros(ctail.shape, F32)

    @pl.when((pl.program_id(0) == 0) & (pl.program_id(1) == 0))
    def _():
        src = lax.broadcasted_iota(jnp.int32, (tg, tg), 0)
        dst = lax.broadcasted_iota(jnp.int32, (tg, tg), 1)
        g_, v_, s_ = dst // CHUNK, (dst % CHUNK) // SUBLANES, dst % SUBLANES
        perm[...] = jnp.where(src == v_ * Rg + g_ * SUBLANES + s_, 1.0, 0.0).astype(BF16)

    h3 = pltpu.einshape("jvc->vjc", x_ref[0].reshape(R, PHASES, D))
    for v in range(PHASES):
        xf_ref[0, :, v * D:(v + 1) * D] = h3[v]
    groups = range(n_split)
    rows = [slice(i * Rg, (i + 1) * Rg) for i in groups]
    hbs = [_rms(h3[:, rows[i], :].reshape(tg, D), g_ref[...]).astype(BF16) for i in groups]

    czs = [jnp.dot(hb, wcfm_ref[...], preferred_element_type=F32) for hb in hbs]
    abs_ = [lax.dot_general(wab_ref[...], hb, NT_DIMS, preferred_element_type=F32) for hb in hbs]
    zs_ = [jnp.dot(hb, wqkvg_ref[...], preferred_element_type=F32) for hb in hbs]

    sub_c = lax.broadcasted_iota(jnp.int32, (1, SUBLANES, cw), 1)
    ctails = [ctail[w] for w in range(PHASES)]
    for i in groups:
        for w in range(PHASES):
            glu = czs[i][w * Rg:(w + 1) * Rg, :cw] * _sigmoid(czs[i][w * Rg:(w + 1) * Rg, cw:])
            cdel[i, 0, w * Rg:(w + 1) * Rg, :] = glu
            for m in range(1, n_del):
                cdel[i, m, w * Rg:(w + 1) * Rg, :] = _delay_rows(glu, ctails[w], m, sub_c)
            ctails[w] = glu[Rg - SUBLANES:, :]
    for w in range(PHASES):
        ctail[w] = ctails[w]
    for i in groups:
        for v in range(PHASES):
            acc = dwb_ref[...]
            for d in range(cfm_k):
                a, e = divmod(d, PHASES)
                w, m = (v - e) % PHASES, a + (1 if v < e else 0)
                acc = acc + dww_ref[cfm_k - 1 - d:cfm_k - d, :] * cdel[i, m, w * Rg:(w + 1) * Rg, :]
            mu = jnp.mean(acc, axis=-1, keepdims=True)
            xc = acc - mu
            var = jnp.mean(xc * xc, axis=-1, keepdims=True)
            c_ref[0, rows[i], v * cw:(v + 1) * cw] = _silu(xc * lax.rsqrt(var + EPS) * lng_ref[...] + lnb_ref[...])

    for i in groups:
        ab = _dot01(abs_[i], perm[...])
        sp_in = ab + dtb_ref[...]
        softplus = jnp.maximum(sp_in, 0.0) + jnp.log(1.0 + jnp.exp(-jnp.abs(sp_in)))
        g = -jnp.exp(alog_ref[...]) * softplus
        row = lax.broadcasted_iota(jnp.int32, ab.shape, 0)
        gb = jnp.where(row < GDN_HEADS, g, _sigmoid(ab))
        for c in range(tg // CHUNK):
            gb_ref[0, i * (tg // CHUNK) + c] = gb[:, c * CHUNK:(c + 1) * CHUNK]

    sub_z = lax.broadcasted_iota(jnp.int32, (1, SUBLANES, qkv_w), 1)
    taps = [convw_ref[j:j + 1, :] for j in range(kc)]
    tails = {w: ztail[w - first_tail] for w in range(first_tail, PHASES)}
    for i in groups:
        z = zs_[i]
        zs = [z[v * Rg:(v + 1) * Rg, :qkv_w] for v in range(PHASES)]
        for v, acc in enumerate(_phase_conv(zs, tails, taps, sub_z)):
            qkv = _silu(acc)
            gate_ref[0, rows[i], v * W:(v + 1) * W] = _silu(z[v * Rg:(v + 1) * Rg, qkv_w:])
            for hd in range(GDN_HEADS):
                lo = hd * GDN_HEAD_DIM
                qh = qkv[:, lo:lo + GDN_HEAD_DIM]
                kh = qkv[:, W + lo:W + lo + GDN_HEAD_DIM]
                q_ref[0, rows[i], v * W + lo:v * W + lo + GDN_HEAD_DIM] = qh * (
                    lax.rsqrt(jnp.sum(qh * qh, axis=-1, keepdims=True) + EPS) * q_scale)
                k_ref[0, rows[i], v * W + lo:v * W + lo + GDN_HEAD_DIM] = kh * lax.rsqrt(
                    jnp.sum(kh * kh, axis=-1, keepdims=True) + EPS)
            v_ref[0, rows[i], v * W:(v + 1) * W] = qkv[:, 2 * W:]
        tails = {w: zs[w][Rg - SUBLANES:, :] for w in range(first_tail, PHASES)}
    for w in range(first_tail, PHASES):
        ztail[w - first_tail] = tails[w]


def _mixer_in(x, g0, w_qkvg, w_ab, w_cfm, conv_w, a_log8, dt_bias8, dw_w, dw_b, ln_g, ln_b, *, tm):
    B, T, D = x.shape
    cfm_k, cw = dw_w.shape
    n_split = 2
    assert GDN_SHORT_CONV - 1 < PHASES and tm % (n_split * CHUNK) == 0
    n_del = (cfm_k - 1) // PHASES + 2
    assert n_del - 1 < SUBLANES
    R = tm // PHASES
    row_spec = lambda w: pl.BlockSpec((1, R, PHASES * w), lambda b, i: (b, i, 0))
    kernel = functools.partial(_mixer_in_kernel, tm=tm, cfm_k=cfm_k)
    wide = jax.ShapeDtypeStruct((B, T // PHASES, PHASES * GDN_WIDTH), F32)
    return pl.pallas_call(
        kernel,
        grid=(B, T // tm),
        in_specs=[pl.BlockSpec((1, tm, D), lambda b, i: (b, i, 0)), _const_spec(g0.shape), _const_spec(w_qkvg.shape), _const_spec(w_ab.shape),
                  _const_spec(w_cfm.shape), _const_spec(conv_w.shape), _const_spec(a_log8.shape),
                  _const_spec(dt_bias8.shape), _const_spec(dw_w.shape), _const_spec(dw_b.shape),
                  _const_spec(ln_g.shape), _const_spec(ln_b.shape)],
        out_specs=[row_spec(GDN_WIDTH), row_spec(GDN_WIDTH), row_spec(GDN_WIDTH), row_spec(GDN_WIDTH),
                   pl.BlockSpec((1, tm // CHUNK, 2 * GDN_HEADS, CHUNK), lambda b, i: (b, i, 0, 0)),
                   row_spec(cw), row_spec(D)],
        out_shape=[wide, wide, wide, wide,
                   jax.ShapeDtypeStruct((B, T // CHUNK, 2 * GDN_HEADS, CHUNK), F32),
                   jax.ShapeDtypeStruct((B, T // PHASES, PHASES * cw), F32),
                   jax.ShapeDtypeStruct((B, T // PHASES, PHASES * D), F32)],
        scratch_shapes=[pltpu.VMEM((GDN_SHORT_CONV - 1, SUBLANES, 3 * GDN_WIDTH), F32),
                        pltpu.VMEM((PHASES, SUBLANES, cw), F32),
                        pltpu.VMEM((n_split, n_del, tm // n_split, cw), F32),
                        pltpu.VMEM((tm // n_split, tm // n_split), BF16)],
        compiler_params=_params(2),
        name="mixer_in",
    )(x, g0, w_qkvg, w_ab, w_cfm, conv_w, a_log8, dt_bias8, dw_w, dw_b, ln_g, ln_b)


def _block_diag(x4, lane_masks):
    xb = x4.astype(BF16)
    return jnp.concatenate([xb * m for m in lane_masks], axis=0)


def _packed_lower_inverse(Ls, eye, diag_mask, off_masks, lane_masks):
    mmp = lambda a, bd: jnp.dot(a.astype(BF16), bd, preferred_element_type=F32)
    Ps = [jnp.where(diag_mask, L, 0.0) for L in Ls]
    Ts = [eye - P for P in Ps]
    for _ in range(3):
        Ps = [mmp(P, _block_diag(P, lane_masks)) for P in Ps]
        Bs = [_block_diag(P, lane_masks) for P in Ps]
        Ts = [T + mmp(T, B) for T, B in zip(Ts, Bs)]
    for m in off_masks:
        Xs = [mmp(T, _block_diag(jnp.where(m, L, 0.0), lane_masks)) for T, L in zip(Ts, Ls)]
        Ts = [T - mmp(X, _block_diag(T, lane_masks)) for T, X in zip(Ts, Xs)]
    return Ts


def _gdn_kernel(q_ref, k_ref, v_ref, gate_ref, gb_ref, ng_ref, o_ref, s_ref, *, n_chunks):
    @pl.when(pl.program_id(0) == 0)
    def _():
        s_ref[...] = jnp.zeros(s_ref.shape, F32)

    C, H, Dh = CHUNK, GDN_HEADS, GDN_HEAD_DIM
    W = H * Dh
    rows_per_chunk = C // PHASES
    nb = q_ref.shape[0]
    ri = lax.broadcasted_iota(jnp.int32, (C, H * C), 0)
    li = lax.broadcasted_iota(jnp.int32, (C, H * C), 1)
    lb, ci = li // C, li % C
    ti = PHASES * (ri % rows_per_chunk) + ri // rows_per_chunk
    tj = PHASES * (ci % rows_per_chunk) + ci // rows_per_chunk
    causal = ti >= tj
    strict = ti > tj
    eye = jnp.where(ri == ci, 1.0, 0.0).astype(F32)
    diag_mask = (ti // 16) == (tj // 16)
    off_masks = [((ti // (2 * s)) == (tj // (2 * s))) & ((ti // s) > (tj // s)) for s in (16, 32)]
    lane_masks = [jnp.where(lb == u, 1.0, 0.0).astype(BF16) for u in range(H)]
    lower1 = jnp.where(causal[:, :C], 1.0, 0.0).astype(BF16)
    upper4 = jnp.where(ti <= tj, 1.0, 0.0).astype(BF16)
    eye1 = eye[:, :C].astype(BF16)
    lb_row = lb[0:1, :]
    ng = ng_ref[...]
    zero_k = jnp.zeros((C, Dh), BF16)

    def load(ref, b, c, h):
        r0 = c * rows_per_chunk
        return jnp.concatenate([ref[b, r0:r0 + rows_per_chunk, p * W + h * Dh:p * W + (h + 1) * Dh]
                                for p in range(PHASES)], axis=0)

    def pick_head(parts):
        out = parts[H - 1]
        for u in range(H - 2, -1, -1):
            out = jnp.where(lb == u, parts[u], out)
        return out

    groups = [(c, b) for c in range(n_chunks) for b in range(nb)]
    A4s, KKs, decays, rhss, qgss, kdecss, elasts = [], [], [], [], [], [], []
    for c, b in groups:
        gbc = gb_ref[b, c]
        gc_rows = _dot01(gbc, upper4)
        gc_col = _dot01(gbc, lower1, NT_DIMS)
        gb_col = _dot01(gbc, eye1, NT_DIMS)
        grow4 = gc_rows[H - 1:H, :]
        for u in range(H - 2, -1, -1):
            grow4 = jnp.where(lb_row == u, gc_rows[u:u + 1, :], grow4)
        gcol4 = pick_head([gc_col[:, u:u + 1] for u in range(H)])
        decays.append(jnp.where(causal, jnp.exp(jnp.where(causal, gcol4 - grow4, 0.0)), 0.0))
        kbqs, kks, rhs, qgs, kdecs, els = [], [], [], [], [], []
        for h in range(H):
            qh = load(q_ref, b, c, h)
            kh = load(k_ref, b, c, h)
            vh = load(v_ref, b, c, h)
            gcol = gc_col[:, h:h + 1]
            bcol = gb_col[:, H + h:H + h + 1]
            g_last = gc_rows[h:h + 1, C - 1:C]
            egc = jnp.exp(gcol)
            kb = kh * bcol
            kbqs.append(jnp.concatenate([kb, qh], axis=0).astype(BF16))
            kb16 = kh.astype(BF16)
            kks.append(jnp.concatenate([kb16 if u == h else zero_k for u in range(H)], axis=1))
            rhs.append(jnp.concatenate([vh * bcol, kb * egc], axis=1).astype(BF16))
            qgs.append(qh * egc)
            kdecs.append(kh * jnp.exp(g_last - gcol))
            els.append(jnp.exp(g_last))
        A4s.append(jnp.concatenate(kbqs, axis=1))
        KKs.append(jnp.concatenate(kks, axis=0))
        rhss.append(jnp.concatenate(rhs, axis=0))
        qgss.append(qgs)
        kdecss.append(kdecs)
        elasts.append(els)
    prods = [lax.dot_general(a, kk, NT_DIMS, preferred_element_type=F32) for a, kk in zip(A4s, KKs)]
    Ls = [jnp.where(strict, p[:C] * d, 0.0) for p, d in zip(prods, decays)]
    attns = [_block_diag(p[C:] * d, lane_masks) for p, d in zip(prods, decays)]
    Ts = _packed_lower_inverse(Ls, eye, diag_mask, off_masks, lane_masks)
    sols = [jnp.dot(_block_diag(T, lane_masks), r, preferred_element_type=F32) for T, r in zip(Ts, rhss)]

    for c in range(n_chunks):
        gi = [c * nb + b for b in range(nb)]
        Ss = [[s_ref[b * H + h] for h in range(H)] for b in range(nb)]
        ws_qs = [[_mm(jnp.concatenate([sols[g][h * C:(h + 1) * C, Dh:], qgss[g][h]], axis=0), Ss[b][h])
                  for h in range(H)] for b, g in enumerate(gi)]
        v_new = [[(sols[g][h * C:(h + 1) * C, :Dh] - ws_qs[b][h][:C]).astype(BF16) for h in range(H)]
                 for b, g in enumerate(gi)]
        o_att = [jnp.dot(attns[g], jnp.concatenate(v_new[b], axis=0), preferred_element_type=F32)
                 for b, g in enumerate(gi)]
        for b, g in enumerate(gi):
            for h in range(H):
                s_ref[b * H + h] = Ss[b][h] * elasts[g][h] + _mm_tn(kdecss[g][h], v_new[b][h])
        r0 = c * rows_per_chunk
        for b, g in enumerate(gi):
            for h in range(H):
                o = ws_qs[b][h][C:] + o_att[b][h * C:(h + 1) * C]
                on = _rms(o, ng) * load(gate_ref, b, c, h)
                for p in range(PHASES):
                    o_ref[b, r0:r0 + rows_per_chunk, p * W + h * Dh:p * W + (h + 1) * Dh] = (
                        on[p * rows_per_chunk:(p + 1) * rows_per_chunk])


def _gdn(q, k, v, gate, gb, norm_g, *, tg):
    B, TP, WP = q.shape
    n_chunks = tg // CHUNK
    row_spec = pl.BlockSpec((B, tg // PHASES, WP), lambda i: (0, i, 0))
    return pl.pallas_call(
        functools.partial(_gdn_kernel, n_chunks=n_chunks),
        grid=(TP * PHASES // tg,),
        in_specs=[row_spec, row_spec, row_spec, row_spec,
                  pl.BlockSpec((B, n_chunks, 2 * GDN_HEADS, CHUNK), lambda i: (0, i, 0, 0)),
                  _const_spec(norm_g.shape)],
        out_specs=row_spec,
        out_shape=jax.ShapeDtypeStruct((B, TP, WP), F32),
        scratch_shapes=[pltpu.VMEM((B * GDN_HEADS, GDN_HEAD_DIM, GDN_HEAD_DIM), F32)],
        compiler_params=_params(1),
        name="gdn",
    )(q, k, v, gate, gb, norm_g)


def _attn_kernel(x_ref, og_ref, c_ref, g_ref, wout_ref, kv_ref, wq_ref, wo_ref, h_ref, *, n_split):
    D = x_ref.shape[2] // PHASES
    R = x_ref.shape[1]
    dh = D // XATTN_HEADS
    per = PHASES // n_split
    groups = [range(i * per, (i + 1) * per) for i in range(n_split)]
    hs = range(XATTN_HEADS)

    def rows(ref, grp):
        w = ref.shape[2] // PHASES
        return jnp.concatenate([ref[0, :, v * w:(v + 1) * w] for v in grp], axis=0)

    mixes = [jnp.concatenate([rows(og_ref, grp).astype(BF16), rows(c_ref, grp).astype(BF16)], axis=1)
             for grp in groups]
    ys = [jnp.dot(m, wout_ref[...], preferred_element_type=F32) for m in mixes]
    h1s = [rows(x_ref, grp) + _rms(y, g_ref[1:2, :]) for grp, y in zip(groups, ys)]
    hns = [_rms(h1, g_ref[2:3, :]).astype(BF16) for h1 in h1s]
    qs = [(jnp.dot(hn, wq_ref[...], preferred_element_type=F32) * (dh ** -0.5)).astype(BF16) for hn in hns]
    ss = [[_mm_nt(q[:, hd * dh:(hd + 1) * dh], kv_ref[0, :, hd * dh:(hd + 1) * dh]) for hd in hs] for q in qs]
    es = [[jnp.exp(s - jnp.max(s, axis=-1, keepdims=True)) for s in sg] for sg in ss]
    pvs = [[_mm(e, kv_ref[0, :, D + hd * dh:D + (hd + 1) * dh]) for e, hd in zip(eg, hs)] for eg in es]
    os_ = [jnp.concatenate([o * (1.0 / jnp.sum(e, axis=-1, keepdims=True)) for o, e in zip(pg, eg)], axis=1)
           for pg, eg in zip(pvs, es)]
    y2s = [_mm(o, wo_ref[...]) for o in os_]
    for grp, h1, y2 in zip(groups, h1s, y2s):
        out = h1 + _rms(y2, g_ref[3:4, :])
        for i, v in enumerate(grp):
            h_ref[0, :, v * D:(v + 1) * D] = out[i * R:(i + 1) * R, :]


def _attn(x, o_gdn, c, norm_g, w_out, kv, w_q, w_o, *, tm):
    B, TP, DP = x.shape
    M = kv.shape[1]
    row_spec = lambda a: pl.BlockSpec((1, tm // PHASES, a.shape[2]), lambda b, i: (b, i, 0))
    return pl.pallas_call(
        functools.partial(_attn_kernel, n_split=2),
        grid=(B, TP * PHASES // tm),
        in_specs=[row_spec(x), row_spec(o_gdn), row_spec(c), _const_spec(norm_g.shape),
                  _const_spec(w_out.shape), pl.BlockSpec((1, M, kv.shape[2]), lambda b, i: (b, 0, 0)),
                  _const_spec(w_q.shape), _const_spec(w_o.shape)],
        out_specs=row_spec(x),
        out_shape=jax.ShapeDtypeStruct((B, TP, DP), F32),
        compiler_params=_params(2),
        name="attn",
    )(x, o_gdn, c, norm_g, w_out, kv, w_q, w_o)


def _ffn_kernel(h_ref, g_ref, wup_ref, cw_ref, cb_ref, wdn_ref, o_ref, carry, *, tm, fc, conv_k):
    F = wdn_ref.shape[0]
    D = wdn_ref.shape[1]
    n_chunks = F // fc
    R = tm // PHASES
    first_tail = PHASES - (conv_k - 1)

    @pl.when(pl.program_id(1) == 0)
    def _():
        carry[...] = jnp.zeros(carry.shape, F32)

    n_split = 2
    Rg = R // n_split
    groups = range(n_split)
    rows = [slice(i * Rg, (i + 1) * Rg) for i in groups]
    slabs_in = _phase_slabs(h_ref, D)
    h2s = [jnp.concatenate([sl[rows[i], :] for sl in slabs_in], axis=0) for i in groups]
    hns = [_rms(h2, g_ref[4:5, :]).astype(BF16) for h2 in h2s]
    sub = lax.broadcasted_iota(jnp.int32, (1, SUBLANES, fc), 1)

    def up(j):
        return [[jnp.dot(hn, wup_ref[:, lo:lo + fc], preferred_element_type=F32) for lo in (j * fc, F + j * fc)]
                for hn in hns]

    def conv(us, lo):
        tails = {w: carry[w - first_tail, :, lo:lo + fc] for w in range(first_tail, PHASES)}
        taps = [cw_ref[k:k + 1, lo:lo + fc] for k in range(conv_k)]
        outs = []
        for u in us:
            slabs = [u[v * Rg:(v + 1) * Rg, :] for v in range(PHASES)]
            outs.append(jnp.concatenate(_phase_conv(slabs, tails, taps, sub, bias=cb_ref[:, lo:lo + fc]), axis=0))
            tails = {w: slabs[w][Rg - SUBLANES:, :] for w in range(first_tail, PHASES)}
        for w in range(first_tail, PHASES):
            carry[w - first_tail, :, lo:lo + fc] = tails[w]
        return outs

    acts = [[] for _ in groups]
    nxt = up(0)
    for j in range(n_chunks):
        cur = nxt
        if j + 1 < n_chunks:
            nxt = up(j + 1)
        ca = conv([cur[i][0] for i in groups], j * fc)
        cb = conv([cur[i][1] for i in groups], F + j * fc)
        for i in groups:
            acts[i].append((_silu(ca[i]) * cb[i]).astype(BF16))
    accs = [jnp.dot(jnp.concatenate(acts[i], axis=1), wdn_ref[...], preferred_element_type=F32) for i in groups]
    outs = [h2s[i] + _rms(accs[i], g_ref[5:6, :]) for i in groups]
    out3 = jnp.concatenate([o.reshape(PHASES, Rg, D) for o in outs], axis=1)
    o_ref[0] = pltpu.einshape("vjc->jvc", out3).reshape(tm, D)


def _ffn(h, norm_g, w_up, conv_w, conv_b, w_down, *, tm, fc):
    B, TP, DP = h.shape
    T, D = TP * PHASES, DP // PHASES
    F = w_down.shape[0]
    conv_k = conv_w.shape[0]
    assert conv_k - 1 < PHASES and tm % (PHASES * SUBLANES) == 0
    row_spec = pl.BlockSpec((1, tm // PHASES, PHASES * D), lambda b, i: (b, i, 0))
    kernel = functools.partial(_ffn_kernel, tm=tm, fc=fc, conv_k=conv_k)
    return pl.pallas_call(
        kernel,
        grid=(B, T // tm),
        in_specs=[row_spec, _const_spec(norm_g.shape), _const_spec(w_up.shape), _const_spec(conv_w.shape),
                  _const_spec(conv_b.shape), _const_spec(w_down.shape)],
        out_specs=pl.BlockSpec((1, tm, D), lambda b, i: (b, i, 0)),
        out_shape=jax.ShapeDtypeStruct((B, T, D), F32),
        scratch_shapes=[pltpu.VMEM((conv_k - 1, SUBLANES, 2 * F), F32)],
        compiler_params=_params(2),
        name="ffn",
    )(h, norm_g, w_up, conv_w, conv_b, w_down)


def _pick_tile(T, pref):
    t = min(T, pref)
    assert T % t == 0 and t % CHUNK == 0, (T, t)
    return t


def _layer(h, mem, g, w_in, gdn_conv_w, a_log, dt_bias, gdn_norm_g, cfm_dw_w, cfm_dw_b, cfm_ln_g, cfm_ln_b,
           w_out, mem_norm_g, xa_w_q, xa_w_kv, xa_w_o, ffn_w_up, ffn_conv_w, ffn_conv_b, ffn_w_down):
    B, T, D = h.shape
    s2 = 4 * GDN_WIDTH
    s4 = s2 + 2 * GDN_HEADS
    w_qkvg = w_in[:, :s2].astype(BF16)
    w_ab = w_in[:, s2:s4].T.astype(BF16)
    w_cfm = w_in[:, s4:].astype(BF16)
    pad = jnp.zeros((GDN_HEADS,), F32)
    a_log8 = jnp.concatenate([a_log.astype(F32), pad])[:, None]
    dt_bias8 = jnp.concatenate([dt_bias.astype(F32), pad])[:, None]

    tm = _pick_tile(T, 512)
    kv = _mem_kv(mem, mem_norm_g[None, :], xa_w_kv.astype(BF16))
    q, k, v, gate, gb, c, hf = _mixer_in(h, g[0:1], w_qkvg, w_ab, w_cfm, gdn_conv_w, a_log8, dt_bias8,
                                         cfm_dw_w, cfm_dw_b[None, :], cfm_ln_g[None, :], cfm_ln_b[None, :], tm=tm)
    o_gdn = _gdn(q, k, v, gate, gb, gdn_norm_g[None, :], tg=_pick_tile(T, 4 * CHUNK))
    hf = _attn(hf, o_gdn, c, g, w_out.astype(BF16), kv, xa_w_q.astype(BF16), xa_w_o.astype(BF16), tm=tm)
    return _ffn(hf, g, ffn_w_up.astype(BF16), ffn_conv_w, ffn_conv_b[None, :], ffn_w_down.astype(BF16),
                tm=tm, fc=256)


def kernel(x, mem, norm_g, w_in, gdn_conv_w, gdn_a_log, gdn_dt_bias, gdn_norm_g, cfm_dw_w, cfm_dw_b,
           cfm_ln_g, cfm_ln_b, w_out, mem_norm_g, xa_w_q, xa_w_kv, xa_w_o, ffn_w_up, ffn_conv_w,
           ffn_conv_b, ffn_w_down):
    h = x
    for l in range(norm_g.shape[0]):
        h = _layer(h, mem, norm_g[l], w_in[l], gdn_conv_w[l], gdn_a_log[l], gdn_dt_bias[l], gdn_norm_g[l],
                   cfm_dw_w[l], cfm_dw_b[l], cfm_ln_g[l], cfm_ln_b[l], w_out[l], mem_norm_g[l],
                   xa_w_q[l], xa_w_kv[l], xa_w_o[l], ffn_w_up[l], ffn_conv_w[l], ffn_conv_b[l], ffn_w_down[l])
    return h
```

```python
import functools

import jax
import jax.numpy as jnp
from jax import lax
from jax.experimental import pallas as pl
from jax.experimental.pallas import tpu as pltpu

EPS = 1e-6
GDN_HEADS = 4
GDN_HEAD_DIM = 128
GDN_WIDTH = GDN_HEADS * GDN_HEAD_DIM
GDN_SHORT_CONV = 4
CHUNK = 64
XATTN_HEADS = 4

F32 = jnp.float32
BF16 = jnp.bfloat16

SUBLANES = 8
PHASES = SUBLANES
VMEM_LIMIT_BYTES = 56 * 1024 * 1024

NT_DIMS = (((1,), (1,)), ((), ()))
TN_DIMS = (((0,), (0,)), ((), ()))


def _mm(a, b):
    return jnp.dot(a.astype(BF16), b.astype(BF16), preferred_element_type=F32)


def _mm_nt(a, b):
    return lax.dot_general(a.astype(BF16), b.astype(BF16), NT_DIMS, preferred_element_type=F32)


def _mm_tn(a, b):
    return lax.dot_general(a.astype(BF16), b.astype(BF16), TN_DIMS, preferred_element_type=F32)


def _split3(x):
    hi = x.astype(BF16)
    r1 = x - hi.astype(F32)
    mid = r1.astype(BF16)
    lo = (r1 - mid.astype(F32)).astype(BF16)
    return hi, mid, lo


def _dot01(x, m01, dims=None):
    out = None
    for part in _split3(x):
        if dims is None:
            d = jnp.dot(part, m01, preferred_element_type=F32)
        else:
            d = lax.dot_general(m01, part, dims, preferred_element_type=F32)
        out = d if out is None else out + d
    return out


def _rms(x, g):
    return x * lax.rsqrt(jnp.mean(x * x, axis=-1, keepdims=True) + EPS) * g


NEG_LOG2_E = -1.4426950408889634


def _sigmoid(x):
    return 1.0 / (1.0 + jnp.exp2(x * NEG_LOG2_E))


def _silu(x):
    return x * _sigmoid(x)


def _delay_rows(x, tail, s, sub):
    rows, width = x.shape
    groups = jnp.concatenate([tail, x], axis=0).reshape(rows // SUBLANES + 1, SUBLANES, width)
    rot = pltpu.roll(groups, s, 1)
    return jnp.where(sub < s, rot[:-1], rot[1:]).reshape(rows, width)


def _phase_slabs(ref, width):
    return [ref[0, :, v * width:(v + 1) * width] for v in range(PHASES)]


def _phase_conv(slabs, tails, taps, sub, bias=None):
    K = len(taps)
    delayed = {w: _delay_rows(slabs[w], tails[w], 1, sub) for w in range(PHASES - (K - 1), PHASES)}
    outs = []
    for v in range(PHASES):
        acc = taps[K - 1] * slabs[v]
        if bias is not None:
            acc = acc + bias
        for d in range(1, K):
            acc = acc + taps[K - 1 - d] * (slabs[v - d] if v >= d else delayed[v - d + PHASES])
        outs.append(acc)
    return outs


def _const_spec(shape):
    zeros = (0,) * len(shape)
    return pl.BlockSpec(shape, lambda *_: zeros)


def _params(n_axes):
    return pltpu.CompilerParams(dimension_semantics=("arbitrary",) * n_axes,
                                vmem_limit_bytes=VMEM_LIMIT_BYTES)


def _mem_kv_kernel(mem_ref, g_ref, w_ref, kv_ref):
    mn = _rms(mem_ref[0], g_ref[...])
    kv_ref[0] = _mm(mn, w_ref[...]).astype(BF16)


def _mem_kv(mem, g, w_kv):
    B, M, D = mem.shape
    N = w_kv.shape[1]
    return pl.pallas_call(
        _mem_kv_kernel,
        grid=(B,),
        in_specs=[pl.BlockSpec((1, M, D), lambda b: (b, 0, 0)), _const_spec((1, D)), _const_spec((D, N))],
        out_specs=pl.BlockSpec((1, M, N), lambda b: (b, 0, 0)),
        out_shape=jax.ShapeDtypeStruct((B, M, N), BF16),
        compiler_params=_params(1),
        name="mem_kv",
    )(mem, g, w_kv)


def _mixer_in_kernel(x_ref, g_ref, wqkvg_ref, wab_ref, wcfm_ref, convw_ref, alog_ref, dtb_ref,
                     dww_ref, dwb_ref, lng_ref, lnb_ref,
                     q_ref, k_ref, v_ref, gate_ref, gb_ref, c_ref, xf_ref,
                     ztail, ctail, cdel, perm, *, tm, cfm_k):
    qkv_w = 3 * GDN_WIDTH
    W = GDN_WIDTH
    D = x_ref.shape[2]
    cw = c_ref.shape[2] // PHASES
    n_split = cdel.shape[0]
    R = tm // PHASES
    Rg = R // n_split
    tg = tm // n_split
    kc = GDN_SHORT_CONV
    first_tail = PHASES - (kc - 1)
    n_del = cdel.shape[1]
    q_scale = GDN_HEAD_DIM ** -0.5

    @pl.when(pl.program_id(1) == 0)
    def _():
        ztail[...] = jnp.zeros(ztail.shape, F32)
        ctail[...] = jnp.zeros(ctail.shape, F32)

    @pl.when((pl.program_id(0) == 0) & (pl.program_id(1) == 0))
    def _():
        src = lax.broadcasted_iota(jnp.int32, (tg, tg), 0)
        dst = lax.broadcasted_iota(jnp.int32, (tg, tg), 1)
        g_, v_, s_ = dst // CHUNK, (dst % CHUNK) // SUBLANES, dst % SUBLANES
        perm[...] = jnp.where(src == v_ * Rg + g_ * SUBLANES + s_, 1.0, 0.0).astype(BF16)

    h3 = pltpu.einshape("jvc->vjc", x_ref[0].reshape(R, PHASES, D))
    for v in range(PHASES):
        xf_ref[0, :, v * D:(v + 1) * D] = h3[v]
    groups = range(n_split)
    rows = [slice(i * Rg, (i + 1) * Rg) for i in groups]
    hbs = [_rms(h3[:, rows[i], :].reshape(tg, D), g_ref[...]).astype(BF16) for i in groups]

    czs = [jnp.dot(hb, wcfm_ref[...], preferred_element_type=F32) for hb in hbs]
    abs_ = [lax.dot_general(wab_ref[...], hb, NT_DIMS, preferred_element_type=F32) for hb in hbs]
    zs_ = [jnp.dot(hb, wqkvg_ref[...], preferred_element_type=F32) for hb in hbs]

    sub_c = lax.broadcasted_iota(jnp.int32, (1, SUBLANES, cw), 1)
    ctails = [ctail[w] for w in range(PHASES)]
    for i in groups:
        for w in range(PHASES):
            glu = czs[i][w * Rg:(w + 1) * Rg, :cw] * _sigmoid(czs[i][w * Rg:(w + 1) * Rg, cw:])
            cdel[i, 0, w * Rg:(w + 1) * Rg, :] = glu
            for m in range(1, n_del):
                cdel[i, m, w * Rg:(w + 1) * Rg, :] = _delay_rows(glu, ctails[w], m, sub_c)
            ctails[w] = glu[Rg - SUBLANES:, :]
    for w in range(PHASES):
        ctail[w] = ctails[w]
    for i in groups:
        for v in range(PHASES):
            acc = dwb_ref[...]
            for d in range(cfm_k):
                a, e = divmod(d, PHASES)
                w, m = (v - e) % PHASES, a + (1 if v < e else 0)
                acc = acc + dww_ref[cfm_k - 1 - d:cfm_k - d, :] * cdel[i, m, w * Rg:(w + 1) * Rg, :]
            mu = jnp.mean(acc, axis=-1, keepdims=True)
            xc = acc - mu
            var = jnp.mean(xc * xc, axis=-1, keepdims=True)
            c_ref[0, rows[i], v * cw:(v + 1) * cw] = _silu(xc * lax.rsqrt(var + EPS) * lng_ref[...] + lnb_ref[...])

    for i in groups:
        ab = _dot01(abs_[i], perm[...])
        sp_in = ab + dtb_ref[...]
        softplus = jnp.maximum(sp_in, 0.0) + jnp.log(1.0 + jnp.exp(-jnp.abs(sp_in)))
        g = -jnp.exp(alog_ref[...]) * softplus
        row = lax.broadcasted_iota(jnp.int32, ab.shape, 0)
        gb = jnp.where(row < GDN_HEADS, g, _sigmoid(ab))
        for c in range(tg // CHUNK):
            gb_ref[0, i * (tg // CHUNK) + c] = gb[:, c * CHUNK:(c + 1) * CHUNK]

    sub_z = lax.broadcasted_iota(jnp.int32, (1, SUBLANES, qkv_w), 1)
    taps = [convw_ref[j:j + 1, :] for j in range(kc)]
    tails = {w: ztail[w - first_tail] for w in range(first_tail, PHASES)}
    for i in groups:
        z = zs_[i]
        zs = [z[v * Rg:(v + 1) * Rg, :qkv_w] for v in range(PHASES)]
        for v, acc in enumerate(_phase_conv(zs, tails, taps, sub_z)):
            qkv = _silu(acc)
            gate_ref[0, rows[i], v * W:(v + 1) * W] = _silu(z[v * Rg:(v + 1) * Rg, qkv_w:])
            for hd in range(GDN_HEADS):
                lo = hd * GDN_HEAD_DIM
                qh = qkv[:, lo:lo + GDN_HEAD_DIM]
                kh = qkv[:, W + lo:W + lo + GDN_HEAD_DIM]
                q_ref[0, rows[i], v * W + lo:v * W + lo + GDN_HEAD_DIM] = qh * (
                    lax.rsqrt(jnp.sum(qh * qh, axis=-1, keepdims=True) + EPS) * q_scale)
                k_ref[0, rows[i], v * W + lo:v * W + lo + GDN_HEAD_DIM] = kh * lax.rsqrt(
                    jnp.sum(kh * kh, axis=-1, keepdims=True) + EPS)
            v_ref[0, rows[i], v * W:(v + 1) * W] = qkv[:, 2 * W:]
        tails = {w: zs[w][Rg - SUBLANES:, :] for w in range(first_tail, PHASES)}
    for w in range(first_tail, PHASES):
        ztail[w - first_tail] = tails[w]


def _mixer_in(x, g0, w_qkvg, w_ab, w_cfm, conv_w, a_log8, dt_bias8, dw_w, dw_b, ln_g, ln_b, *, tm):
    B, T, D = x.shape
    cfm_k, cw = dw_w.shape
    n_split = 2
    assert GDN_SHORT_CONV - 1 < PHASES and tm % (n_split * CHUNK) == 0
    n_del = (cfm_k - 1) // PHASES + 2
    assert n_del - 1 < SUBLANES
    R = tm // PHASES
    row_spec = lambda w: pl.BlockSpec((1, R, PHASES * w), lambda b, i: (b, i, 0))
    kernel = functools.partial(_mixer_in_kernel, tm=tm, cfm_k=cfm_k)
    wide = jax.ShapeDtypeStruct((B, T // PHASES, PHASES * GDN_WIDTH), F32)
    return pl.pallas_call(
        kernel,
        grid=(B, T // tm),
        in_specs=[pl.BlockSpec((1, tm, D), lambda b, i: (b, i, 0)), _const_spec(g0.shape), _const_spec(w_qkvg.shape), _const_spec(w_ab.shape),
                  _const_spec(w_cfm.shape), _const_spec(conv_w.shape), _const_spec(a_log8.shape),
                  _const_spec(dt_bias8.shape), _const_spec(dw_w.shape), _const_spec(dw_b.shape),
                  _const_spec(ln_g.shape), _const_spec(ln_b.shape)],
        out_specs=[row_spec(GDN_WIDTH), row_spec(GDN_WIDTH), row_spec(GDN_WIDTH), row_spec(GDN_WIDTH),
                   pl.BlockSpec((1, tm // CHUNK, 2 * GDN_HEADS, CHUNK), lambda b, i: (b, i, 0, 0)),
                   row_spec(cw), row_spec(D)],
        out_shape=[wide, wide, wide, wide,
                   jax.ShapeDtypeStruct((B, T // CHUNK, 2 * GDN_HEADS, CHUNK), F32),
                   jax.ShapeDtypeStruct((B, T // PHASES, PHASES * cw), F32),
                   jax.ShapeDtypeStruct((B, T // PHASES, PHASES * D), F32)],
        scratch_shapes=[pltpu.VMEM((GDN_SHORT_CONV - 1, SUBLANES, 3 * GDN_WIDTH), F32),
                        pltpu.VMEM((PHASES, SUBLANES, cw), F32),
                        pltpu.VMEM((n_split, n_del, tm // n_split, cw), F32),
                        pltpu.VMEM((tm // n_split, tm // n_split), BF16)],
        compiler_params=_params(2),
        name="mixer_in",
    )(x, g0, w_qkvg, w_ab, w_cfm, conv_w, a_log8, dt_bias8, dw_w, dw_b, ln_g, ln_b)


def _block_diag(x4, lane_masks):
    xb = x4.astype(BF16)
    return jnp.concatenate([xb * m for m in lane_masks], axis=0)


def _packed_lower_inverse(Ls, eye, diag_mask, off_masks, lane_masks):
    mmp = lambda a, bd: jnp.dot(a.astype(BF16), bd, preferred_element_type=F32)
    Ps = [jnp.where(diag_mask, L, 0.0) for L in Ls]
    Ts = [eye - P for P in Ps]
    for _ in range(3):
        Ps = [mmp(P, _block_diag(P, lane_masks)) for P in Ps]
        Bs = [_block_diag(P, lane_masks) for P in Ps]
        Ts = [T + mmp(T, B) for T, B in zip(Ts, Bs)]
    for m in off_masks:
        Xs = [mmp(T, _block_diag(jnp.where(m, L, 0.0), lane_masks)) for T, L in zip(Ts, Ls)]
        Ts = [T - mmp(X, _block_diag(T, lane_masks)) for T, X in zip(Ts, Xs)]
    return Ts


def _gdn_kernel(q_ref, k_ref, v_ref, gate_ref, gb_ref, ng_ref, o_ref, s_ref, *, n_chunks):
    @pl.when(pl.program_id(0) == 0)
    def _():
        s_ref[...] = jnp.zeros(s_ref.shape, F32)

    C, H, Dh = CHUNK, GDN_HEADS, GDN_HEAD_DIM
    W = H * Dh
    rows_per_chunk = C // PHASES
    nb = q_ref.shape[0]
    ri = lax.broadcasted_iota(jnp.int32, (C, H * C), 0)
    li = lax.broadcasted_iota(jnp.int32, (C, H * C), 1)
    lb, ci = li // C, li % C
    ti = PHASES * (ri % rows_per_chunk) + ri // rows_per_chunk
    tj = PHASES * (ci % rows_per_chunk) + ci // rows_per_chunk
    causal = ti >= tj
    strict = ti > tj
    eye = jnp.where(ri == ci, 1.0, 0.0).astype(F32)
    diag_mask = (ti // 16) == (tj // 16)
    off_masks = [((ti // (2 * s)) == (tj // (2 * s))) & ((ti // s) > (tj // s)) for s in (16, 32)]
    lane_masks = [jnp.where(lb == u, 1.0, 0.0).astype(BF16) for u in range(H)]
    lower1 = jnp.where(causal[:, :C], 1.0, 0.0).astype(BF16)
    upper4 = jnp.where(ti <= tj, 1.0, 0.0).astype(BF16)
    eye1 = eye[:, :C].astype(BF16)
    lb_row = lb[0:1, :]
    ng = ng_ref[...]
    zero_k = jnp.zeros((C, Dh), BF16)

    def load(ref, b, c, h):
        r0 = c * rows_per_chunk
        return jnp.concatenate([ref[b, r0:r0 + rows_per_chunk, p * W + h * Dh:p * W + (h + 1) * Dh]
                                for p in range(PHASES)], axis=0)

    def pick_head(parts):
        out = parts[H - 1]
        for u in range(H - 2, -1, -1):
            out = jnp.where(lb == u, parts[u], out)
        return out

    groups = [(c, b) for c in range(n_chunks) for b in range(nb)]
    A4s, KKs, decays, rhss, qgss, kdecss, elasts = [], [], [], [], [], [], []
    for c, b in groups:
        gbc = gb_ref[b, c]
        gc_rows = _dot01(gbc, upper4)
        gc_col = _dot01(gbc, lower1, NT_DIMS)
        gb_col = _dot01(gbc, eye1, NT_DIMS)
        grow4 = gc_rows[H - 1:H, :]
        for u in range(H - 2, -1, -1):
            grow4 = jnp.where(lb_row == u, gc_rows[u:u + 1, :], grow4)
        gcol4 = pick_head([gc_col[:, u:u + 1] for u in range(H)])
        decays.append(jnp.where(causal, jnp.exp(jnp.where(causal, gcol4 - grow4, 0.0)), 0.0))
        kbqs, kks, rhs, qgs, kdecs, els = [], [], [], [], [], []
        for h in range(H):
            qh = load(q_ref, b, c, h)
            kh = load(k_ref, b, c, h)
            vh = load(v_ref, b, c, h)
            gcol = gc_col[:, h:h + 1]
            bcol = gb_col[:, H + h:H + h + 1]
            g_last = gc_rows[h:h + 1, C - 1:C]
            egc = jnp.exp(gcol)
            kb = kh * bcol
            kbqs.append(jnp.concatenate([kb, qh], axis=0).astype(BF16))
            kb16 = kh.astype(BF16)
            kks.append(jnp.concatenate([kb16 if u == h else zero_k for u in range(H)], axis=1))
            rhs.append(jnp.concatenate([vh * bcol, kb * egc], axis=1).astype(BF16))
            qgs.append(qh * egc)
            kdecs.append(kh * jnp.exp(g_last - gcol))
            els.append(jnp.exp(g_last))
        A4s.append(jnp.concatenate(kbqs, axis=1))
        KKs.append(jnp.concatenate(kks, axis=0))
        rhss.append(jnp.concatenate(rhs, axis=0))
        qgss.append(qgs)
        kdecss.append(kdecs)
        elasts.append(els)
    prods = [lax.dot_general(a, kk, NT_DIMS, preferred_element_type=F32) for a, kk in zip(A4s, KKs)]
    Ls = [jnp.where(strict, p[:C] * d, 0.0) for p, d in zip(prods, decays)]
    attns = [_block_diag(p[C:] * d, lane_masks) for p, d in zip(prods, decays)]
    Ts = _packed_lower_inverse(Ls, eye, diag_mask, off_masks, lane_masks)
    sols = [jnp.dot(_block_diag(T, lane_masks), r, preferred_element_type=F32) for T, r in zip(Ts, rhss)]

    for c in range(n_chunks):
        gi = [c * nb + b for b in range(nb)]
        Ss = [[s_ref[b * H + h] for h in range(H)] for b in range(nb)]
        ws_qs = [[_mm(jnp.concatenate([sols[g][h * C:(h + 1) * C, Dh:], qgss[g][h]], axis=0), Ss[b][h])
                  for h in range(H)] for b, g in enumerate(gi)]
        v_new = [[(sols[g][h * C:(h + 1) * C, :Dh] - ws_qs[b][h][:C]).astype(BF16) for h in range(H)]
                 for b, g in enumerate(gi)]
        o_att = [jnp.dot(attns[g], jnp.concatenate(v_new[b], axis=0), preferred_element_type=F32)
                 for b, g in enumerate(gi)]
        for b, g in enumerate(gi):
            for h in range(H):
                s_ref[b * H + h] = Ss[b][h] * elasts[g][h] + _mm_tn(kdecss[g][h], v_new[b][h])
        r0 = c * rows_per_chunk
        for b, g in enumerate(gi):
            for h in range(H):
                o = ws_qs[b][h][C:] + o_att[b][h * C:(h + 1) * C]
                on = _rms(o, ng) * load(gate_ref, b, c, h)
                for p in range(PHASES):
                    o_ref[b, r0:r0 + rows_per_chunk, p * W + h * Dh:p * W + (h + 1) * Dh] = (
                        on[p * rows_per_chunk:(p + 1) * rows_per_chunk])


def _gdn(q, k, v, gate, gb, norm_g, *, tg):
    B, TP, WP = q.shape
    n_chunks = tg // CHUNK
    row_spec = pl.BlockSpec((B, tg // PHASES, WP), lambda i: (0, i, 0))
    return pl.pallas_call(
        functools.partial(_gdn_kernel, n_chunks=n_chunks),
        grid=(TP * PHASES // tg,),
        in_specs=[row_spec, row_spec, row_spec, row_spec,
                  pl.BlockSpec((B, n_chunks, 2 * GDN_HEADS, CHUNK), lambda i: (0, i, 0, 0)),
                  _const_spec(norm_g.shape)],
        out_specs=row_spec,
        out_shape=jax.ShapeDtypeStruct((B, TP, WP), F32),
        scratch_shapes=[pltpu.VMEM((B * GDN_HEADS, GDN_HEAD_DIM, GDN_HEAD_DIM), F32)],
        compiler_params=_params(1),
        name="gdn",
    )(q, k, v, gate, gb, norm_g)


def _attn_kernel(x_ref, og_ref, c_ref, g_ref, wout_ref, kv_ref, wq_ref, wo_ref, h_ref, *, n_split):
    D = x_ref.shape[2] // PHASES
    R = x_ref.shape[1]
    dh = D // XATTN_HEADS
    per = PHASES // n_split
    groups = [range(i * per, (i + 1) * per) for i in range(n_split)]
    hs = range(XATTN_HEADS)

    def rows(ref, grp):
        w = ref.shape[2] // PHASES
        return jnp.concatenate([ref[0, :, v * w:(v + 1) * w] for v in grp], axis=0)

    mixes = [jnp.concatenate([rows(og_ref, grp).astype(BF16), rows(c_ref, grp).astype(BF16)], axis=1)
             for grp in groups]
    ys = [jnp.dot(m, wout_ref[...], preferred_element_type=F32) for m in mixes]
    h1s = [rows(x_ref, grp) + _rms(y, g_ref[1:2, :]) for grp, y in zip(groups, ys)]
    hns = [_rms(h1, g_ref[2:3, :]).astype(BF16) for h1 in h1s]
    qs = [(jnp.dot(hn, wq_ref[...], preferred_element_type=F32) * (dh ** -0.5)).astype(BF16) for hn in hns]
    ss = [[_mm_nt(q[:, hd * dh:(hd + 1) * dh], kv_ref[0, :, hd * dh:(hd + 1) * dh]) for hd in hs] for q in qs]
    es = [[jnp.exp(s - jnp.max(s, axis=-1, keepdims=True)) for s in sg] for sg in ss]
    pvs = [[_mm(e, kv_ref[0, :, D + hd * dh:D + (hd + 1) * dh]) for e, hd in zip(eg, hs)] for eg in es]
    os_ = [jnp.concatenate([o * (1.0 / jnp.sum(e, axis=-1, keepdims=True)) for o, e in zip(pg, eg)], axis=1)
           for pg, eg in zip(pvs, es)]
    y2s = [_mm(o, wo_ref[...]) for o in os_]
    for grp, h1, y2 in zip(groups, h1s, y2s):
        out = h1 + _rms(y2, g_ref[3:4, :])
        for i, v in enumerate(grp):
            h_ref[0, :, v * D:(v + 1) * D] = out[i * R:(i + 1) * R, :]


def _attn(x, o_gdn, c, norm_g, w_out, kv, w_q, w_o, *, tm):
    B, TP, DP = x.shape
    M = kv.shape[1]
    row_spec = lambda a: pl.BlockSpec((1, tm // PHASES, a.shape[2]), lambda b, i: (b, i, 0))
    return pl.pallas_call(
        functools.partial(_attn_kernel, n_split=tm // 256),
        grid=(B, TP * PHASES // tm),
        in_specs=[row_spec(x), row_spec(o_gdn), row_spec(c), _const_spec(norm_g.shape),
                  _const_spec(w_out.shape), pl.BlockSpec((1, M, kv.shape[2]), lambda b, i: (b, 0, 0)),
                  _const_spec(w_q.shape), _const_spec(w_o.shape)],
        out_specs=row_spec(x),
        out_shape=jax.ShapeDtypeStruct((B, TP, DP), F32),
        compiler_params=_params(2),
        name="attn",
    )(x, o_gdn, c, norm_g, w_out, kv, w_q, w_o)


def _ffn_kernel(h_ref, g_ref, wup_ref, cw_ref, cb_ref, wdn_ref, o_ref, carry, *, tm, fc, conv_k):
    F = wdn_ref.shape[0]
    D = wdn_ref.shape[1]
    n_chunks = F // fc
    R = tm // PHASES
    first_tail = PHASES - (conv_k - 1)

    @pl.when(pl.program_id(1) == 0)
    def _():
        carry[...] = jnp.zeros(carry.shape, F32)

    n_split = 2
    Rg = R // n_split
    groups = range(n_split)
    rows = [slice(i * Rg, (i + 1) * Rg) for i in groups]
    slabs_in = _phase_slabs(h_ref, D)
    h2s = [jnp.concatenate([sl[rows[i], :] for sl in slabs_in], axis=0) for i in groups]
    hns = [_rms(h2, g_ref[4:5, :]).astype(BF16) for h2 in h2s]
    sub = lax.broadcasted_iota(jnp.int32, (1, SUBLANES, fc), 1)

    def up(j):
        return [[jnp.dot(hn, wup_ref[:, lo:lo + fc], preferred_element_type=F32) for lo in (j * fc, F + j * fc)]
                for hn in hns]

    def conv(us, lo):
        tails = {w: carry[w - first_tail, :, lo:lo + fc] for w in range(first_tail, PHASES)}
        taps = [cw_ref[k:k + 1, lo:lo + fc] for k in range(conv_k)]
        outs = []
        for u in us:
            slabs = [u[v * Rg:(v + 1) * Rg, :] for v in range(PHASES)]
            outs.append(jnp.concatenate(_phase_conv(slabs, tails, taps, sub, bias=cb_ref[:, lo:lo + fc]), axis=0))
            tails = {w: slabs[w][Rg - SUBLANES:, :] for w in range(first_tail, PHASES)}
        for w in range(first_tail, PHASES):
            carry[w - first_tail, :, lo:lo + fc] = tails[w]
        return outs

    acts = [[] for _ in groups]
    nxt = up(0)
    for j in range(n_chunks):
        cur = nxt
        if j + 1 < n_chunks:
            nxt = up(j + 1)
        ca = conv([cur[i][0] for i in groups], j * fc)
        cb = conv([cur[i][1] for i in groups], F + j * fc)
        for i in groups:
            acts[i].append((_silu(ca[i]) * cb[i]).astype(BF16))
    accs = [jnp.dot(jnp.concatenate(acts[i], axis=1), wdn_ref[...], preferred_element_type=F32) for i in groups]
    outs = [h2s[i] + _rms(accs[i], g_ref[5:6, :]) for i in groups]
    out3 = jnp.concatenate([o.reshape(PHASES, Rg, D) for o in outs], axis=1)
    o_ref[0] = pltpu.einshape("vjc->jvc", out3).reshape(tm, D)


def _ffn(h, norm_g, w_up, conv_w, conv_b, w_down, *, tm, fc):
    B, TP, DP = h.shape
    T, D = TP * PHASES, DP // PHASES
    F = w_down.shape[0]
    conv_k = conv_w.shape[0]
    assert conv_k - 1 < PHASES and tm % (PHASES * SUBLANES) == 0
    row_spec = pl.BlockSpec((1, tm // PHASES, PHASES * D), lambda b, i: (b, i, 0))
    kernel = functools.partial(_ffn_kernel, tm=tm, fc=fc, conv_k=conv_k)
    return pl.pallas_call(
        kernel,
        grid=(B, T // tm),
        in_specs=[row_spec, _const_spec(norm_g.shape), _const_spec(w_up.shape), _const_spec(conv_w.shape),
                  _const_spec(conv_b.shape), _const_spec(w_down.shape)],
        out_specs=pl.BlockSpec((1, tm, D), lambda b, i: (b, i, 0)),
        out_shape=jax.ShapeDtypeStruct((B, T, D), F32),
        scratch_shapes=[pltpu.VMEM((conv_k - 1, SUBLANES, 2 * F), F32)],
        compiler_params=_params(2),
        name="ffn",
    )(h, norm_g, w_up, conv_w, conv_b, w_down)


def _pick_tile(T, pref):
    t = min(T, pref)
    assert T % t == 0 and t % CHUNK == 0, (T, t)
    return t


def _layer(h, mem, g, w_in, gdn_conv_w, a_log, dt_bias, gdn_norm_g, cfm_dw_w, cfm_dw_b, cfm_ln_g, cfm_ln_b,
           w_out, mem_norm_g, xa_w_q, xa_w_kv, xa_w_o, ffn_w_up, ffn_conv_w, ffn_conv_b, ffn_w_down):
    B, T, D = h.shape
    s2 = 4 * GDN_WIDTH
    s4 = s2 + 2 * GDN_HEADS
    w_qkvg = w_in[:, :s2].astype(BF16)
    w_ab = w_in[:, s2:s4].T.astype(BF16)
    w_cfm = w_in[:, s4:].astype(BF16)
    pad = jnp.zeros((GDN_HEADS,), F32)
    a_log8 = jnp.concatenate([a_log.astype(F32), pad])[:, None]
    dt_bias8 = jnp.concatenate([dt_bias.astype(F32), pad])[:, None]

    tm = _pick_tile(T, 512)
    kv = _mem_kv(mem, mem_norm_g[None, :], xa_w_kv.astype(BF16))
    q, k, v, gate, gb, c, hf = _mixer_in(h, g[0:1], w_qkvg, w_ab, w_cfm, gdn_conv_w, a_log8, dt_bias8,
                                         cfm_dw_w, cfm_dw_b[None, :], cfm_ln_g[None, :], cfm_ln_b[None, :], tm=tm)
    o_gdn = _gdn(q, k, v, gate, gb, gdn_norm_g[None, :], tg=_pick_tile(T, 4 * CHUNK))
    hf = _attn(hf, o_gdn, c, g, w_out.astype(BF16), kv, xa_w_q.astype(BF16), xa_w_o.astype(BF16),
               tm=_pick_tile(T, 1024))
    return _ffn(hf, g, ffn_w_up.astype(BF16), ffn_conv_w, ffn_conv_b[None, :], ffn_w_down.astype(BF16),
                tm=tm, fc=256)


def kernel(x, mem, norm_g, w_in, gdn_conv_w, gdn_a_log, gdn_dt_bias, gdn_norm_g, cfm_dw_w, cfm_dw_b,
           cfm_ln_g, cfm_ln_b, w_out, mem_norm_g, xa_w_q, xa_w_kv, xa_w_o, ffn_w_up, ffn_conv_w,
           ffn_conv_b, ffn_w_down):
    h = x
    for l in range(norm_g.shape[0]):
        h = _layer(h, mem, norm_g[l], w_in[l], gdn_conv_w[l], gdn_a_log[l], gdn_dt_bias[l], gdn_norm_g[l],
                   cfm_dw_w[l], cfm_dw_b[l], cfm_ln_g[l], cfm_ln_b[l], w_out[l], mem_norm_g[l],
                   xa_w_q[l], xa_w_kv[l], xa_w_o[l], ffn_w_up[l], ffn_conv_w[l], ffn_conv_b[l], ffn_w_down[l])
    return h
```

```python
import functools

import jax
import jax.numpy as jnp
from jax import lax
from jax.experimental import pallas as pl
from jax.experimental.pallas import tpu as pltpu

EPS = 1e-6
GDN_HEADS = 4
GDN_HEAD_DIM = 128
GDN_WIDTH = GDN_HEADS * GDN_HEAD_DIM
GDN_SHORT_CONV = 4
CHUNK = 64
XATTN_HEADS = 4

F32 = jnp.float32
BF16 = jnp.bfloat16

SUBLANES = 8
PHASES = SUBLANES
VMEM_LIMIT_BYTES = 56 * 1024 * 1024

NT_DIMS = (((1,), (1,)), ((), ()))
TN_DIMS = (((0,), (0,)), ((), ()))


def _mm(a, b):
    return jnp.dot(a.astype(BF16), b.astype(BF16), preferred_element_type=F32)


def _mm_nt(a, b):
    return lax.dot_general(a.astype(BF16), b.astype(BF16), NT_DIMS, preferred_element_type=F32)


def _mm_tn(a, b):
    return lax.dot_general(a.astype(BF16), b.astype(BF16), TN_DIMS, preferred_element_type=F32)


def _split3(x):
    hi = x.astype(BF16)
    r1 = x - hi.astype(F32)
    mid = r1.astype(BF16)
    lo = (r1 - mid.astype(F32)).astype(BF16)
    return hi, mid, lo


def _dot01(x, m01, dims=None):
    out = None
    for part in _split3(x):
        if dims is None:
            d = jnp.dot(part, m01, preferred_element_type=F32)
        else:
            d = lax.dot_general(m01, part, dims, preferred_element_type=F32)
        out = d if out is None else out + d
    return out


def _rms(x, g):
    return x * lax.rsqrt(jnp.mean(x * x, axis=-1, keepdims=True) + EPS) * g


NEG_LOG2_E = -1.4426950408889634


def _sigmoid(x):
    return 1.0 / (1.0 + jnp.exp2(x * NEG_LOG2_E))


def _silu(x):
    return x * _sigmoid(x)


def _delay_rows(x, tail, s, sub):
    rows, width = x.shape
    groups = jnp.concatenate([tail, x], axis=0).reshape(rows // SUBLANES + 1, SUBLANES, width)
    rot = pltpu.roll(groups, s, 1)
    return jnp.where(sub < s, rot[:-1], rot[1:]).reshape(rows, width)


def _phase_slabs(ref, width):
    return [ref[0, :, v * width:(v + 1) * width] for v in range(PHASES)]


def _phase_conv(slabs, tails, taps, sub, bias=None):
    K = len(taps)
    delayed = {w: _delay_rows(slabs[w], tails[w], 1, sub) for w in range(PHASES - (K - 1), PHASES)}
    outs = []
    for v in range(PHASES):
        acc = taps[K - 1] * slabs[v]
        if bias is not None:
            acc = acc + bias
        for d in range(1, K):
            acc = acc + taps[K - 1 - d] * (slabs[v - d] if v >= d else delayed[v - d + PHASES])
        outs.append(acc)
    return outs


def _const_spec(shape):
    zeros = (0,) * len(shape)
    return pl.BlockSpec(shape, lambda *_: zeros)


def _params(n_axes):
    return pltpu.CompilerParams(dimension_semantics=("arbitrary",) * n_axes,
                                vmem_limit_bytes=VMEM_LIMIT_BYTES)


def _mem_kv_kernel(mem_ref, g_ref, w_ref, kv_ref):
    mn = _rms(mem_ref[0], g_ref[...])
    kv_ref[0] = _mm(mn, w_ref[...]).astype(BF16)


def _mem_kv(mem, g, w_kv):
    B, M, D = mem.shape
    N = w_kv.shape[1]
    return pl.pallas_call(
        _mem_kv_kernel,
        grid=(B,),
        in_specs=[pl.BlockSpec((1, M, D), lambda b: (b, 0, 0)), _const_spec((1, D)), _const_spec((D, N))],
        out_specs=pl.BlockSpec((1, M, N), lambda b: (b, 0, 0)),
        out_shape=jax.ShapeDtypeStruct((B, M, N), BF16),
        compiler_params=_params(1),
        name="mem_kv",
    )(mem, g, w_kv)


def _mixer_in_kernel(x_ref, g_ref, wqkvg_ref, wab_ref, wcfm_ref, convw_ref, alog_ref, dtb_ref,
                     dww_ref, dwb_ref, lng_ref, lnb_ref,
                     q_ref, k_ref, v_ref, gate_ref, gb_ref, c_ref, xf_ref,
                     ztail, ctail, cdel, perm, *, tm, cfm_k):
    qkv_w = 3 * GDN_WIDTH
    W = GDN_WIDTH
    D = x_ref.shape[2]
    cw = c_ref.shape[2] // PHASES
    n_split = cdel.shape[0]
    R = tm // PHASES
    Rg = R // n_split
    tg = tm // n_split
    kc = GDN_SHORT_CONV
    first_tail = PHASES - (kc - 1)
    n_del = cdel.shape[1]
    q_scale = GDN_HEAD_DIM ** -0.5

    @pl.when(pl.program_id(1) == 0)
    def _():
        ztail[...] = jnp.zeros(ztail.shape, F32)
        ctail[...] = jnp.zeros(ctail.shape, F32)

    @pl.when((pl.program_id(0) == 0) & (pl.program_id(1) == 0))
    def _():
        src = lax.broadcasted_iota(jnp.int32, (tg, tg), 0)
        dst = lax.broadcasted_iota(jnp.int32, (tg, tg), 1)
        g_, v_, s_ = dst // CHUNK, (dst % CHUNK) // SUBLANES, dst % SUBLANES
        perm[...] = jnp.where(src == v_ * Rg + g_ * SUBLANES + s_, 1.0, 0.0).astype(BF16)

    h3 = pltpu.einshape("jvc->vjc", x_ref[0].reshape(R, PHASES, D))
    for v in range(PHASES):
        xf_ref[0, :, v * D:(v + 1) * D] = h3[v]
    groups = range(n_split)
    rows = [slice(i * Rg, (i + 1) * Rg) for i in groups]
    hbs = [_rms(h3[:, rows[i], :].reshape(tg, D), g_ref[...]).astype(BF16) for i in groups]

    czs = [jnp.dot(hb, wcfm_ref[...], preferred_element_type=F32) for hb in hbs]
    abs_ = [lax.dot_general(wab_ref[...], hb, NT_DIMS, preferred_element_type=F32) for hb in hbs]
    zs_ = [jnp.dot(hb, wqkvg_ref[...], preferred_element_type=F32) for hb in hbs]

    sub_c = lax.broadcasted_iota(jnp.int32, (1, SUBLANES, cw), 1)
    ctails = [ctail[w] for w in range(PHASES)]
    for i in groups:
        for w in range(PHASES):
            glu = czs[i][w * Rg:(w + 1) * Rg, :cw] * _sigmoid(czs[i][w * Rg:(w + 1) * Rg, cw:])
            cdel[i, 0, w * Rg:(w + 1) * Rg, :] = glu
            for m in range(1, n_del):
                cdel[i, m, w * Rg:(w + 1) * Rg, :] = _delay_rows(glu, ctails[w], m, sub_c)
            ctails[w] = glu[Rg - SUBLANES:, :]
    for w in range(PHASES):
        ctail[w] = ctails[w]
    for i in groups:
        for v in range(PHASES):
            acc = dwb_ref[...]
            for d in range(cfm_k):
                a, e = divmod(d, PHASES)
                w, m = (v - e) % PHASES, a + (1 if v < e else 0)
                acc = acc + dww_ref[cfm_k - 1 - d:cfm_k - d, :] * cdel[i, m, w * Rg:(w + 1) * Rg, :]
            mu = jnp.mean(acc, axis=-1, keepdims=True)
            xc = acc - mu
            var = jnp.mean(xc * xc, axis=-1, keepdims=True)
            c_ref[0, rows[i], v * cw:(v + 1) * cw] = _silu(xc * lax.rsqrt(var + EPS) * lng_ref[...] + lnb_ref[...])

    for i in groups:
        ab = _dot01(abs_[i], perm[...])
        sp_in = ab + dtb_ref[...]
        softplus = jnp.maximum(sp_in, 0.0) + jnp.log(1.0 + jnp.exp(-jnp.abs(sp_in)))
        g = -jnp.exp(alog_ref[...]) * softplus
        row = lax.broadcasted_iota(jnp.int32, ab.shape, 0)
        gb = jnp.where(row < GDN_HEADS, g, _sigmoid(ab))
        for c in range(tg // CHUNK):
            gb_ref[0, i * (tg // CHUNK) + c] = gb[:, c * CHUNK:(c + 1) * CHUNK]

    sub_z = lax.broadcasted_iota(jnp.int32, (1, SUBLANES, qkv_w), 1)
    taps = [convw_ref[j:j + 1, :] for j in range(kc)]
    tails = {w: ztail[w - first_tail] for w in range(first_tail, PHASES)}
    for i in groups:
        z = zs_[i]
        zs = [z[v * Rg:(v + 1) * Rg, :qkv_w] for v in range(PHASES)]
        for v, acc in enumerate(_phase_conv(zs, tails, taps, sub_z)):
            qkv = _silu(acc)
            gate_ref[0, rows[i], v * W:(v + 1) * W] = _silu(z[v * Rg:(v + 1) * Rg, qkv_w:])
            for hd in range(GDN_HEADS):
                lo = hd * GDN_HEAD_DIM
                qh = qkv[:, lo:lo + GDN_HEAD_DIM]
                kh = qkv[:, W + lo:W + lo + GDN_HEAD_DIM]
                q_ref[0, rows[i], v * W + lo:v * W + lo + GDN_HEAD_DIM] = qh * (
                    lax.rsqrt(jnp.sum(qh * qh, axis=-1, keepdims=True) + EPS) * q_scale)
                k_ref[0, rows[i], v * W + lo:v * W + lo + GDN_HEAD_DIM] = kh * lax.rsqrt(
                    jnp.sum(kh * kh, axis=-1, keepdims=True) + EPS)
            v_ref[0, rows[i], v * W:(v + 1) * W] = qkv[:, 2 * W:]
        tails = {w: zs[w][Rg - SUBLANES:, :] for w in range(first_tail, PHASES)}
    for w in range(first_tail, PHASES):
        ztail[w - first_tail] = tails[w]


def _mixer_in(x, g0, w_qkvg, w_ab, w_cfm, conv_w, a_log8, dt_bias8, dw_w, dw_b, ln_g, ln_b, *, tm):
    B, T, D = x.shape
    cfm_k, cw = dw_w.shape
    n_split = 2
    assert GDN_SHORT_CONV - 1 < PHASES and tm % (n_split * CHUNK) == 0
    n_del = (cfm_k - 1) // PHASES + 2
    assert n_del - 1 < SUBLANES
    R = tm // PHASES
    row_spec = lambda w: pl.BlockSpec((1, R, PHASES * w), lambda b, i: (b, i, 0))
    kernel = functools.partial(_mixer_in_kernel, tm=tm, cfm_k=cfm_k)
    wide = jax.ShapeDtypeStruct((B, T // PHASES, PHASES * GDN_WIDTH), F32)
    return pl.pallas_call(
        kernel,
        grid=(B, T // tm),
        in_specs=[pl.BlockSpec((1, tm, D), lambda b, i: (b, i, 0)), _const_spec(g0.shape), _const_spec(w_qkvg.shape), _const_spec(w_ab.shape),
                  _const_spec(w_cfm.shape), _const_spec(conv_w.shape), _const_spec(a_log8.shape),
                  _const_spec(dt_bias8.shape), _const_spec(dw_w.shape), _const_spec(dw_b.shape),
                  _const_spec(ln_g.shape), _const_spec(ln_b.shape)],
        out_specs=[row_spec(GDN_WIDTH), row_spec(GDN_WIDTH), row_spec(GDN_WIDTH), row_spec(GDN_WIDTH),
                   pl.BlockSpec((1, tm // CHUNK, 2 * GDN_HEADS, CHUNK), lambda b, i: (b, i, 0, 0)),
                   row_spec(cw), row_spec(D)],
        out_shape=[wide, wide, wide, wide,
                   jax.ShapeDtypeStruct((B, T // CHUNK, 2 * GDN_HEADS, CHUNK), F32),
                   jax.ShapeDtypeStruct((B, T // PHASES, PHASES * cw), F32),
                   jax.ShapeDtypeStruct((B, T // PHASES, PHASES * D), F32)],
        scratch_shapes=[pltpu.VMEM((GDN_SHORT_CONV - 1, SUBLANES, 3 * GDN_WIDTH), F32),
                        pltpu.VMEM((PHASES, SUBLANES, cw), F32),
                        pltpu.VMEM((n_split, n_del, tm // n_split, cw), F32),
                        pltpu.VMEM((tm // n_split, tm // n_split), BF16)],
        compiler_params=_params(2),
        name="mixer_in",
    )(x, g0, w_qkvg, w_ab, w_cfm, conv_w, a_log8, dt_bias8, dw_w, dw_b, ln_g, ln_b)


def _block_diag(x4, lane_masks):
    xb = x4.astype(BF16)
    return jnp.concatenate([xb * m for m in lane_masks], axis=0)


def _packed_lower_inverse(Ls, eye, diag_mask, off_masks, lane_masks):
    mmp = lambda a, bd: jnp.dot(a.astype(BF16), bd, preferred_element_type=F32)
    Ps = [jnp.where(diag_mask, L, 0.0) for L in Ls]
    Ts = [eye - P for P in Ps]
    for _ in range(3):
        Ps = [mmp(P, _block_diag(P, lane_masks)) for P in Ps]
        Bs = [_block_diag(P, lane_masks) for P in Ps]
        Ts = [T + mmp(T, B) for T, B in zip(Ts, Bs)]
    for m in off_masks:
        Xs = [mmp(T, _block_diag(jnp.where(m, L, 0.0), lane_masks)) for T, L in zip(Ts, Ls)]
        Ts = [T - mmp(X, _block_diag(T, lane_masks)) for T, X in zip(Ts, Xs)]
    return Ts


def _gdn_kernel(q_ref, k_ref, v_ref, gate_ref, gb_ref, ng_ref, o_ref, s_ref, *, n_chunks):
    @pl.when(pl.program_id(0) == 0)
    def _():
        s_ref[...] = jnp.zeros(s_ref.shape, F32)

    C, H, Dh = CHUNK, GDN_HEADS, GDN_HEAD_DIM
    W = H * Dh
    rows_per_chunk = C // PHASES
    nb = q_ref.shape[0]
    ri = lax.broadcasted_iota(jnp.int32, (C, H * C), 0)
    li = lax.broadcasted_iota(jnp.int32, (C, H * C), 1)
    lb, ci = li // C, li % C
    ti = PHASES * (ri % rows_per_chunk) + ri // rows_per_chunk
    tj = PHASES * (ci % rows_per_chunk) + ci // rows_per_chunk
    causal = ti >= tj
    strict = ti > tj
    eye = jnp.where(ri == ci, 1.0, 0.0).astype(F32)
    diag_mask = (ti // 16) == (tj // 16)
    off_masks = [((ti // (2 * s)) == (tj // (2 * s))) & ((ti // s) > (tj // s)) for s in (16, 32)]
    lane_masks = [jnp.where(lb == u, 1.0, 0.0).astype(BF16) for u in range(H)]
    lower1 = jnp.where(causal[:, :C], 1.0, 0.0).astype(BF16)
    upper4 = jnp.where(ti <= tj, 1.0, 0.0).astype(BF16)
    eye1 = eye[:, :C].astype(BF16)
    lb_row = lb[0:1, :]
    ng = ng_ref[...]
    zero_k = jnp.zeros((C, Dh), BF16)

    def load(ref, b, c, h):
        r0 = c * rows_per_chunk
        return jnp.concatenate([ref[b, r0:r0 + rows_per_chunk, p * W + h * Dh:p * W + (h + 1) * Dh]
                                for p in range(PHASES)], axis=0)

    def pick_head(parts):
        out = parts[H - 1]
        for u in range(H - 2, -1, -1):
            out = jnp.where(lb == u, parts[u], out)
        return out

    groups = [(c, b) for c in range(n_chunks) for b in range(nb)]
    A4s, KKs, decays, rhss, qgss, kdecss, elasts = [], [], [], [], [], [], []
    for c, b in groups:
        gbc = gb_ref[b, c]
        gc_rows = _dot01(gbc, upper4)
        gc_col = _dot01(gbc, lower1, NT_DIMS)
        gb_col = _dot01(gbc, eye1, NT_DIMS)
        grow4 = gc_rows[H - 1:H, :]
        for u in range(H - 2, -1, -1):
            grow4 = jnp.where(lb_row == u, gc_rows[u:u + 1, :], grow4)
        gcol4 = pick_head([gc_col[:, u:u + 1] for u in range(H)])
        decays.append(jnp.where(causal, jnp.exp(jnp.where(causal, gcol4 - grow4, 0.0)), 0.0))
        kbqs, kks, rhs, qgs, kdecs, els = [], [], [], [], [], []
        for h in range(H):
            qh = load(q_ref, b, c, h)
            kh = load(k_ref, b, c, h)
            vh = load(v_ref, b, c, h)
            gcol = gc_col[:, h:h + 1]
            bcol = gb_col[:, H + h:H + h + 1]
            g_last = gc_rows[h:h + 1, C - 1:C]
            egc = jnp.exp(gcol)
            kb = kh * bcol
            kbqs.append(jnp.concatenate([kb, qh], axis=0).astype(BF16))
            kb16 = kh.astype(BF16)
            kks.append(jnp.concatenate([kb16 if u == h else zero_k for u in range(H)], axis=1))
            rhs.append(jnp.concatenate([vh * bcol, kb * egc], axis=1).astype(BF16))
            qgs.append(qh * egc)
            kdecs.append(kh * jnp.exp(g_last - gcol))
            els.append(jnp.exp(g_last))
        A4s.append(jnp.concatenate(kbqs, axis=1))
        KKs.append(jnp.concatenate(kks, axis=0))
        rhss.append(jnp.concatenate(rhs, axis=0))
        qgss.append(qgs)
        kdecss.append(kdecs)
        elasts.append(els)
    prods = [lax.dot_general(a, kk, NT_DIMS, preferred_element_type=F32) for a, kk in zip(A4s, KKs)]
    Ls = [jnp.where(strict, p[:C] * d, 0.0) for p, d in zip(prods, decays)]
    attns = [_block_diag(p[C:] * d, lane_masks) for p, d in zip(prods, decays)]
    Ts = _packed_lower_inverse(Ls, eye, diag_mask, off_masks, lane_masks)
    sols = [jnp.dot(_block_diag(T, lane_masks), r, preferred_element_type=F32) for T, r in zip(Ts, rhss)]

    for c in range(n_chunks):
        gi = [c * nb + b for b in range(nb)]
        Ss = [[s_ref[b * H + h] for h in range(H)] for b in range(nb)]
        ws_qs = [[_mm(jnp.concatenate([sols[g][h * C:(h + 1) * C, Dh:], qgss[g][h]], axis=0), Ss[b][h])
                  for h in range(H)] for b, g in enumerate(gi)]
        v_new = [[(sols[g][h * C:(h + 1) * C, :Dh] - ws_qs[b][h][:C]).astype(BF16) for h in range(H)]
                 for b, g in enumerate(gi)]
        o_att = [jnp.dot(attns[g], jnp.concatenate(v_new[b], axis=0), preferred_element_type=F32)
                 for b, g in enumerate(gi)]
        for b, g in enumerate(gi):
            for h in range(H):
                s_ref[b * H + h] = Ss[b][h] * elasts[g][h] + _mm_tn(kdecss[g][h], v_new[b][h])
        r0 = c * rows_per_chunk
        for b, g in enumerate(gi):
            for h in range(H):
                o = ws_qs[b][h][C:] + o_att[b][h * C:(h + 1) * C]
                on = _rms(o, ng) * load(gate_ref, b, c, h)
                for p in range(PHASES):
                    o_ref[b, r0:r0 + rows_per_chunk, p * W + h * Dh:p * W + (h + 1) * Dh] = (
                        on[p * rows_per_chunk:(p + 1) * rows_per_chunk])


def _gdn(q, k, v, gate, gb, norm_g, *, tg):
    B, TP, WP = q.shape
    n_chunks = tg // CHUNK
    row_spec = pl.BlockSpec((B, tg // PHASES, WP), lambda i: (0, i, 0))
    return pl.pallas_call(
        functools.partial(_gdn_kernel, n_chunks=n_chunks),
        grid=(TP * PHASES // tg,),
        in_specs=[row_spec, row_spec, row_spec, row_spec,
                  pl.BlockSpec((B, n_chunks, 2 * GDN_HEADS, CHUNK), lambda i: (0, i, 0, 0)),
                  _const_spec(norm_g.shape)],
        out_specs=row_spec,
        out_shape=jax.ShapeDtypeStruct((B, TP, WP), F32),
        scratch_shapes=[pltpu.VMEM((B * GDN_HEADS, GDN_HEAD_DIM, GDN_HEAD_DIM), F32)],
        compiler_params=_params(1),
        name="gdn",
    )(q, k, v, gate, gb, norm_g)


def _attn_kernel(x_ref, og_ref, c_ref, g_ref, wout_ref, kv_ref, wq_ref, wo_ref, h_ref, *, n_split):
    D = x_ref.shape[2] // PHASES
    R = x_ref.shape[1]
    dh = D // XATTN_HEADS
    per = PHASES // n_split
    groups = [range(i * per, (i + 1) * per) for i in range(n_split)]
    hs = range(XATTN_HEADS)

    def rows(ref, grp):
        w = ref.shape[2] // PHASES
        return jnp.concatenate([ref[0, :, v * w:(v + 1) * w] for v in grp], axis=0)

    mixes = [jnp.concatenate([rows(og_ref, grp).astype(BF16), rows(c_ref, grp).astype(BF16)], axis=1)
             for grp in groups]
    ys = [jnp.dot(m, wout_ref[...], preferred_element_type=F32) for m in mixes]
    h1s = [rows(x_ref, grp) + _rms(y, g_ref[1:2, :]) for grp, y in zip(groups, ys)]
    hns = [_rms(h1, g_ref[2:3, :]).astype(BF16) for h1 in h1s]
    qs = [(jnp.dot(hn, wq_ref[...], preferred_element_type=F32) * (dh ** -0.5)).astype(BF16) for hn in hns]
    ss = [[_mm_nt(q[:, hd * dh:(hd + 1) * dh], kv_ref[0, :, hd * dh:(hd + 1) * dh]) for hd in hs] for q in qs]
    es = [[jnp.exp(s - jnp.max(s, axis=-1, keepdims=True)) for s in sg] for sg in ss]
    pvs = [[_mm(e, kv_ref[0, :, D + hd * dh:D + (hd + 1) * dh]) for e, hd in zip(eg, hs)] for eg in es]
    os_ = [jnp.concatenate([o * (1.0 / jnp.sum(e, axis=-1, keepdims=True)) for o, e in zip(pg, eg)], axis=1)
           for pg, eg in zip(pvs, es)]
    y2s = [_mm(o, wo_ref[...]) for o in os_]
    for grp, h1, y2 in zip(groups, h1s, y2s):
        out = h1 + _rms(y2, g_ref[3:4, :])
        for i, v in enumerate(grp):
            h_ref[0, :, v * D:(v + 1) * D] = out[i * R:(i + 1) * R, :]


def _attn(x, o_gdn, c, norm_g, w_out, kv, w_q, w_o, *, tm):
    B, TP, DP = x.shape
    M = kv.shape[1]
    row_spec = lambda a: pl.BlockSpec((1, tm // PHASES, a.shape[2]), lambda b, i: (b, i, 0))
    return pl.pallas_call(
        functools.partial(_attn_kernel, n_split=tm // 256),
        grid=(B, TP * PHASES // tm),
        in_specs=[row_spec(x), row_spec(o_gdn), row_spec(c), _const_spec(norm_g.shape),
                  _const_spec(w_out.shape), pl.BlockSpec((1, M, kv.shape[2]), lambda b, i: (b, 0, 0)),
                  _const_spec(w_q.shape), _const_spec(w_o.shape)],
        out_specs=row_spec(x),
        out_shape=jax.ShapeDtypeStruct((B, TP, DP), F32),
        compiler_params=_params(2),
        name="attn",
    )(x, o_gdn, c, norm_g, w_out, kv, w_q, w_o)


def _ffn_kernel(h_ref, g_ref, wup_ref, cw_ref, cb_ref, wdn_ref, o_ref, carry, *, tm, fc, conv_k):
    F = wdn_ref.shape[0]
    D = wdn_ref.shape[1]
    n_chunks = F // fc
    R = tm // PHASES
    first_tail = PHASES - (conv_k - 1)

    @pl.when(pl.program_id(1) == 0)
    def _():
        carry[...] = jnp.zeros(carry.shape, F32)

    n_split = tm // 256
    Rg = R // n_split
    groups = range(n_split)
    rows = [slice(i * Rg, (i + 1) * Rg) for i in groups]
    slabs_in = _phase_slabs(h_ref, D)
    h2s = [jnp.concatenate([sl[rows[i], :] for sl in slabs_in], axis=0) for i in groups]
    hns = [_rms(h2, g_ref[4:5, :]).astype(BF16) for h2 in h2s]
    sub = lax.broadcasted_iota(jnp.int32, (1, SUBLANES, fc), 1)

    def up(j):
        return [[jnp.dot(hn, wup_ref[:, lo:lo + fc], preferred_element_type=F32) for lo in (j * fc, F + j * fc)]
                for hn in hns]

    def conv(us, lo):
        tails = {w: carry[w - first_tail, :, lo:lo + fc] for w in range(first_tail, PHASES)}
        taps = [cw_ref[k:k + 1, lo:lo + fc] for k in range(conv_k)]
        outs = []
        for u in us:
            slabs = [u[v * Rg:(v + 1) * Rg, :] for v in range(PHASES)]
            outs.append(jnp.concatenate(_phase_conv(slabs, tails, taps, sub, bias=cb_ref[:, lo:lo + fc]), axis=0))
            tails = {w: slabs[w][Rg - SUBLANES:, :] for w in range(first_tail, PHASES)}
        for w in range(first_tail, PHASES):
            carry[w - first_tail, :, lo:lo + fc] = tails[w]
        return outs

    acts = [[] for _ in groups]
    nxt = up(0)
    for j in range(n_chunks):
        cur = nxt
        if j + 1 < n_chunks:
            nxt = up(j + 1)
        ca = conv([cur[i][0] for i in groups], j * fc)
        cb = conv([cur[i][1] for i in groups], F + j * fc)
        for i in groups:
            acts[i].append((_silu(ca[i]) * cb[i]).astype(BF16))
    accs = [jnp.dot(jnp.concatenate(acts[i], axis=1), wdn_ref[...], preferred_element_type=F32) for i in groups]
    outs = [h2s[i] + _rms(accs[i], g_ref[5:6, :]) for i in groups]
    out3 = jnp.concatenate([o.reshape(PHASES, Rg, D) for o in outs], axis=1)
    o_ref[0] = pltpu.einshape("vjc->jvc", out3).reshape(tm, D)


def _ffn(h, norm_g, w_up, conv_w, conv_b, w_down, *, tm, fc):
    B, TP, DP = h.shape
    T, D = TP * PHASES, DP // PHASES
    F = w_down.shape[0]
    conv_k = conv_w.shape[0]
    assert conv_k - 1 < PHASES and tm % (PHASES * SUBLANES) == 0
    row_spec = pl.BlockSpec((1, tm // PHASES, PHASES * D), lambda b, i: (b, i, 0))
    kernel = functools.partial(_ffn_kernel, tm=tm, fc=fc, conv_k=conv_k)
    return pl.pallas_call(
        kernel,
        grid=(B, T // tm),
        in_specs=[row_spec, _const_spec(norm_g.shape), _const_spec(w_up.shape), _const_spec(conv_w.shape),
                  _const_spec(conv_b.shape), _const_spec(w_down.shape)],
        out_specs=pl.BlockSpec((1, tm, D), lambda b, i: (b, i, 0)),
        out_shape=jax.ShapeDtypeStruct((B, T, D), F32),
        scratch_shapes=[pltpu.VMEM((conv_k - 1, SUBLANES, 2 * F), F32)],
        compiler_params=_params(2),
        name="ffn",
    )(h, norm_g, w_up, conv_w, conv_b, w_down)


def _pick_tile(T, pref):
    t = min(T, pref)
    assert T % t == 0 and t % CHUNK == 0, (T, t)
    return t


def _layer(h, mem, g, w_in, gdn_conv_w, a_log, dt_bias, gdn_norm_g, cfm_dw_w, cfm_dw_b, cfm_ln_g, cfm_ln_b,
           w_out, mem_norm_g, xa_w_q, xa_w_kv, xa_w_o, ffn_w_up, ffn_conv_w, ffn_conv_b, ffn_w_down):
    B, T, D = h.shape
    s2 = 4 * GDN_WIDTH
    s4 = s2 + 2 * GDN_HEADS
    w_qkvg = w_in[:, :s2].astype(BF16)
    w_ab = w_in[:, s2:s4].T.astype(BF16)
    w_cfm = w_in[:, s4:].astype(BF16)
    pad = jnp.zeros((GDN_HEADS,), F32)
    a_log8 = jnp.concatenate([a_log.astype(F32), pad])[:, None]
    dt_bias8 = jnp.concatenate([dt_bias.astype(F32), pad])[:, None]

    tm = _pick_tile(T, 512)
    kv = _mem_kv(mem, mem_norm_g[None, :], xa_w_kv.astype(BF16))
    q, k, v, gate, gb, c, hf = _mixer_in(h, g[0:1], w_qkvg, w_ab, w_cfm, gdn_conv_w, a_log8, dt_bias8,
                                         cfm_dw_w, cfm_dw_b[None, :], cfm_ln_g[None, :], cfm_ln_b[None, :], tm=tm)
    o_gdn = _gdn(q, k, v, gate, gb, gdn_norm_g[None, :], tg=_pick_tile(T, 4 * CHUNK))
    hf = _attn(hf, o_gdn, c, g, w_out.astype(BF16), kv, xa_w_q.astype(BF16), xa_w_o.astype(BF16),
               tm=_pick_tile(T, 1024))
    return _ffn(hf, g, ffn_w_up.astype(BF16), ffn_conv_w, ffn_conv_b[None, :], ffn_w_down.astype(BF16),
                tm=_pick_tile(T, 1024), fc=256)


def kernel(x, mem, norm_g, w_in, gdn_conv_w, gdn_a_log, gdn_dt_bias, gdn_norm_g, cfm_dw_w, cfm_dw_b,
           cfm_ln_g, cfm_ln_b, w_out, mem_norm_g, xa_w_q, xa_w_kv, xa_w_o, ffn_w_up, ffn_conv_w,
           ffn_conv_b, ffn_w_down):
    h = x
    for l in range(norm_g.shape[0]):
        h = _layer(h, mem, norm_g[l], w_in[l], gdn_conv_w[l], gdn_a_log[l], gdn_dt_bias[l], gdn_norm_g[l],
                   cfm_dw_w[l], cfm_dw_b[l], cfm_ln_g[l], cfm_ln_b[l], w_out[l], mem_norm_g[l],
                   xa_w_q[l], xa_w_kv[l], xa_w_o[l], ffn_w_up[l], ffn_conv_w[l], ffn_conv_b[l], ffn_w_down[l])
    return h
```

```python
import functools

import jax
import jax.numpy as jnp
from jax import lax
from jax.experimental import pallas as pl
from jax.experimental.pallas import tpu as pltpu

EPS = 1e-6
GDN_HEADS = 4
GDN_HEAD_DIM = 128
GDN_WIDTH = GDN_HEADS * GDN_HEAD_DIM
GDN_SHORT_CONV = 4
CHUNK = 64
XATTN_HEADS = 4

F32 = jnp.float32
BF16 = jnp.bfloat16

SUBLANES = 8
PHASES = SUBLANES
VMEM_LIMIT_BYTES = 56 * 1024 * 1024

MIXER_TILE = 512
TOKEN_TILE = 1024
GDN_TILE = 4 * CHUNK
ROW_GROUP = 256
FFN_CHUNK = 256

NT_DIMS = (((1,), (1,)), ((), ()))
TN_DIMS = (((0,), (0,)), ((), ()))


def _mm(a, b):
    return jnp.dot(a.astype(BF16), b.astype(BF16), preferred_element_type=F32)


def _mm_nt(a, b):
    return lax.dot_general(a.astype(BF16), b.astype(BF16), NT_DIMS, preferred_element_type=F32)


def _mm_tn(a, b):
    return lax.dot_general(a.astype(BF16), b.astype(BF16), TN_DIMS, preferred_element_type=F32)


def _split3(x):
    hi = x.astype(BF16)
    r1 = x - hi.astype(F32)
    mid = r1.astype(BF16)
    lo = (r1 - mid.astype(F32)).astype(BF16)
    return hi, mid, lo


def _dot01(x, m01, dims=None):
    out = None
    for part in _split3(x):
        if dims is None:
            d = jnp.dot(part, m01, preferred_element_type=F32)
        else:
            d = lax.dot_general(m01, part, dims, preferred_element_type=F32)
        out = d if out is None else out + d
    return out


def _rms(x, g):
    return x * lax.rsqrt(jnp.mean(x * x, axis=-1, keepdims=True) + EPS) * g


NEG_LOG2_E = -1.4426950408889634


def _sigmoid(x):
    return 1.0 / (1.0 + jnp.exp2(x * NEG_LOG2_E))


def _silu(x):
    return x * _sigmoid(x)


def _delay_rows(x, tail, s, sub):
    rows, width = x.shape
    groups = jnp.concatenate([tail, x], axis=0).reshape(rows // SUBLANES + 1, SUBLANES, width)
    rot = pltpu.roll(groups, s, 1)
    return jnp.where(sub < s, rot[:-1], rot[1:]).reshape(rows, width)


def _phase_slabs(ref, width):
    return [ref[0, :, v * width:(v + 1) * width] for v in range(PHASES)]


def _phase_conv(slabs, tails, taps, sub, bias=None):
    K = len(taps)
    delayed = {w: _delay_rows(slabs[w], tails[w], 1, sub) for w in range(PHASES - (K - 1), PHASES)}
    outs = []
    for v in range(PHASES):
        acc = taps[K - 1] * slabs[v]
        if bias is not None:
            acc = acc + bias
        for d in range(1, K):
            acc = acc + taps[K - 1 - d] * (slabs[v - d] if v >= d else delayed[v - d + PHASES])
        outs.append(acc)
    return outs


def _const_spec(shape):
    zeros = (0,) * len(shape)
    return pl.BlockSpec(shape, lambda *_: zeros)


def _params(n_axes):
    return pltpu.CompilerParams(dimension_semantics=("arbitrary",) * n_axes,
                                vmem_limit_bytes=VMEM_LIMIT_BYTES)


def _mem_kv_kernel(mem_ref, g_ref, w_ref, kv_ref):
    mn = _rms(mem_ref[0], g_ref[...])
    kv_ref[0] = _mm(mn, w_ref[...]).astype(BF16)


def _mem_kv(mem, g, w_kv):
    B, M, D = mem.shape
    N = w_kv.shape[1]
    return pl.pallas_call(
        _mem_kv_kernel,
        grid=(B,),
        in_specs=[pl.BlockSpec((1, M, D), lambda b: (b, 0, 0)), _const_spec((1, D)), _const_spec((D, N))],
        out_specs=pl.BlockSpec((1, M, N), lambda b: (b, 0, 0)),
        out_shape=jax.ShapeDtypeStruct((B, M, N), BF16),
        compiler_params=_params(1),
        name="mem_kv",
    )(mem, g, w_kv)


def _mixer_in_kernel(x_ref, g_ref, wqkvg_ref, wab_ref, wcfm_ref, convw_ref, alog_ref, dtb_ref,
                     dww_ref, dwb_ref, lng_ref, lnb_ref,
                     q_ref, k_ref, v_ref, gate_ref, gb_ref, c_ref, xf_ref,
                     ztail, ctail, cdel, perm, *, tm, cfm_k):
    qkv_w = 3 * GDN_WIDTH
    W = GDN_WIDTH
    D = x_ref.shape[2]
    cw = c_ref.shape[2] // PHASES
    n_split = cdel.shape[0]
    R = tm // PHASES
    Rg = R // n_split
    tg = tm // n_split
    kc = GDN_SHORT_CONV
    first_tail = PHASES - (kc - 1)
    n_del = cdel.shape[1]
    q_scale = GDN_HEAD_DIM ** -0.5

    @pl.when(pl.program_id(1) == 0)
    def _():
        ztail[...] = jnp.zeros(ztail.shape, F32)
        ctail[...] = jnp.zeros(ctail.shape, F32)

    @pl.when((pl.program_id(0) == 0) & (pl.program_id(1) == 0))
    def _():
        src = lax.broadcasted_iota(jnp.int32, (tg, tg), 0)
        dst = lax.broadcasted_iota(jnp.int32, (tg, tg), 1)
        g_, v_, s_ = dst // CHUNK, (dst % CHUNK) // SUBLANES, dst % SUBLANES
        perm[...] = jnp.where(src == v_ * Rg + g_ * SUBLANES + s_, 1.0, 0.0).astype(BF16)

    h3 = pltpu.einshape("jvc->vjc", x_ref[0].reshape(R, PHASES, D))
    for v in range(PHASES):
        xf_ref[0, :, v * D:(v + 1) * D] = h3[v]
    groups = range(n_split)
    rows = [slice(i * Rg, (i + 1) * Rg) for i in groups]
    hbs = [_rms(h3[:, rows[i], :].reshape(tg, D), g_ref[...]).astype(BF16) for i in groups]

    czs = [jnp.dot(hb, wcfm_ref[...], preferred_element_type=F32) for hb in hbs]
    abs_ = [lax.dot_general(wab_ref[...], hb, NT_DIMS, preferred_element_type=F32) for hb in hbs]
    zs_ = [jnp.dot(hb, wqkvg_ref[...], preferred_element_type=F32) for hb in hbs]

    sub_c = lax.broadcasted_iota(jnp.int32, (1, SUBLANES, cw), 1)
    ctails = [ctail[w] for w in range(PHASES)]
    for i in groups:
        for w in range(PHASES):
            glu = czs[i][w * Rg:(w + 1) * Rg, :cw] * _sigmoid(czs[i][w * Rg:(w + 1) * Rg, cw:])
            cdel[i, 0, w * Rg:(w + 1) * Rg, :] = glu
            for m in range(1, n_del):
                cdel[i, m, w * Rg:(w + 1) * Rg, :] = _delay_rows(glu, ctails[w], m, sub_c)
            ctails[w] = glu[Rg - SUBLANES:, :]
    for w in range(PHASES):
        ctail[w] = ctails[w]
    for i in groups:
        for v in range(PHASES):
            acc = dwb_ref[...]
            for d in range(cfm_k):
                a, e = divmod(d, PHASES)
                w, m = (v - e) % PHASES, a + (1 if v < e else 0)
                acc = acc + dww_ref[cfm_k - 1 - d:cfm_k - d, :] * cdel[i, m, w * Rg:(w + 1) * Rg, :]
            mu = jnp.mean(acc, axis=-1, keepdims=True)
            xc = acc - mu
            var = jnp.mean(xc * xc, axis=-1, keepdims=True)
            c_ref[0, rows[i], v * cw:(v + 1) * cw] = _silu(xc * lax.rsqrt(var + EPS) * lng_ref[...] + lnb_ref[...])

    for i in groups:
        ab = _dot01(abs_[i], perm[...])
        sp_in = ab + dtb_ref[...]
        softplus = jnp.maximum(sp_in, 0.0) + jnp.log(1.0 + jnp.exp(-jnp.abs(sp_in)))
        g = -jnp.exp(alog_ref[...]) * softplus
        row = lax.broadcasted_iota(jnp.int32, ab.shape, 0)
        gb = jnp.where(row < GDN_HEADS, g, _sigmoid(ab))
        for c in range(tg // CHUNK):
            gb_ref[0, i * (tg // CHUNK) + c] = gb[:, c * CHUNK:(c + 1) * CHUNK]

    sub_z = lax.broadcasted_iota(jnp.int32, (1, SUBLANES, qkv_w), 1)
    taps = [convw_ref[j:j + 1, :] for j in range(kc)]
    tails = {w: ztail[w - first_tail] for w in range(first_tail, PHASES)}
    for i in groups:
        z = zs_[i]
        zs = [z[v * Rg:(v + 1) * Rg, :qkv_w] for v in range(PHASES)]
        for v, acc in enumerate(_phase_conv(zs, tails, taps, sub_z)):
            qkv = _silu(acc)
            gate_ref[0, rows[i], v * W:(v + 1) * W] = _silu(z[v * Rg:(v + 1) * Rg, qkv_w:])
            for hd in range(GDN_HEADS):
                lo = hd * GDN_HEAD_DIM
                qh = qkv[:, lo:lo + GDN_HEAD_DIM]
                kh = qkv[:, W + lo:W + lo + GDN_HEAD_DIM]
                q_ref[0, rows[i], v * W + lo:v * W + lo + GDN_HEAD_DIM] = qh * (
                    lax.rsqrt(jnp.sum(qh * qh, axis=-1, keepdims=True) + EPS) * q_scale)
                k_ref[0, rows[i], v * W + lo:v * W + lo + GDN_HEAD_DIM] = kh * lax.rsqrt(
                    jnp.sum(kh * kh, axis=-1, keepdims=True) + EPS)
            v_ref[0, rows[i], v * W:(v + 1) * W] = qkv[:, 2 * W:]
        tails = {w: zs[w][Rg - SUBLANES:, :] for w in range(first_tail, PHASES)}
    for w in range(first_tail, PHASES):
        ztail[w - first_tail] = tails[w]


def _mixer_in(x, g0, w_qkvg, w_ab, w_cfm, conv_w, a_log8, dt_bias8, dw_w, dw_b, ln_g, ln_b, *, tm):
    B, T, D = x.shape
    cfm_k, cw = dw_w.shape
    n_split = tm // ROW_GROUP
    assert GDN_SHORT_CONV - 1 < PHASES and tm % (n_split * CHUNK) == 0
    n_del = (cfm_k - 1) // PHASES + 2
    assert n_del - 1 < SUBLANES
    R = tm // PHASES
    row_spec = lambda w: pl.BlockSpec((1, R, PHASES * w), lambda b, i: (b, i, 0))
    kernel = functools.partial(_mixer_in_kernel, tm=tm, cfm_k=cfm_k)
    wide = jax.ShapeDtypeStruct((B, T // PHASES, PHASES * GDN_WIDTH), F32)
    return pl.pallas_call(
        kernel,
        grid=(B, T // tm),
        in_specs=[pl.BlockSpec((1, tm, D), lambda b, i: (b, i, 0)), _const_spec(g0.shape), _const_spec(w_qkvg.shape), _const_spec(w_ab.shape),
                  _const_spec(w_cfm.shape), _const_spec(conv_w.shape), _const_spec(a_log8.shape),
                  _const_spec(dt_bias8.shape), _const_spec(dw_w.shape), _const_spec(dw_b.shape),
                  _const_spec(ln_g.shape), _const_spec(ln_b.shape)],
        out_specs=[row_spec(GDN_WIDTH), row_spec(GDN_WIDTH), row_spec(GDN_WIDTH), row_spec(GDN_WIDTH),
                   pl.BlockSpec((1, tm // CHUNK, 2 * GDN_HEADS, CHUNK), lambda b, i: (b, i, 0, 0)),
                   row_spec(cw), row_spec(D)],
        out_shape=[wide, wide, wide, wide,
                   jax.ShapeDtypeStruct((B, T // CHUNK, 2 * GDN_HEADS, CHUNK), F32),
                   jax.ShapeDtypeStruct((B, T // PHASES, PHASES * cw), F32),
                   jax.ShapeDtypeStruct((B, T // PHASES, PHASES * D), F32)],
        scratch_shapes=[pltpu.VMEM((GDN_SHORT_CONV - 1, SUBLANES, 3 * GDN_WIDTH), F32),
                        pltpu.VMEM((PHASES, SUBLANES, cw), F32),
                        pltpu.VMEM((n_split, n_del, tm // n_split, cw), F32),
                        pltpu.VMEM((tm // n_split, tm // n_split), BF16)],
        compiler_params=_params(2),
        name="mixer_in",
    )(x, g0, w_qkvg, w_ab, w_cfm, conv_w, a_log8, dt_bias8, dw_w, dw_b, ln_g, ln_b)


def _block_diag(x4, lane_masks):
    xb = x4.astype(BF16)
    return jnp.concatenate([xb * m for m in lane_masks], axis=0)


def _packed_lower_inverse(Ls, eye, diag_mask, off_masks, lane_masks):
    mmp = lambda a, bd: jnp.dot(a.astype(BF16), bd, preferred_element_type=F32)
    Ps = [jnp.where(diag_mask, L, 0.0) for L in Ls]
    Ts = [eye - P for P in Ps]
    for _ in range(3):
        Ps = [mmp(P, _block_diag(P, lane_masks)) for P in Ps]
        Bs = [_block_diag(P, lane_masks) for P in Ps]
        Ts = [T + mmp(T, B) for T, B in zip(Ts, Bs)]
    for m in off_masks:
        Xs = [mmp(T, _block_diag(jnp.where(m, L, 0.0), lane_masks)) for T, L in zip(Ts, Ls)]
        Ts = [T - mmp(X, _block_diag(T, lane_masks)) for T, X in zip(Ts, Xs)]
    return Ts


def _gdn_kernel(q_ref, k_ref, v_ref, gate_ref, gb_ref, ng_ref, o_ref, s_ref, *, n_chunks):
    @pl.when(pl.program_id(0) == 0)
    def _():
        s_ref[...] = jnp.zeros(s_ref.shape, F32)

    C, H, Dh = CHUNK, GDN_HEADS, GDN_HEAD_DIM
    W = H * Dh
    rows_per_chunk = C // PHASES
    nb = q_ref.shape[0]
    ri = lax.broadcasted_iota(jnp.int32, (C, H * C), 0)
    li = lax.broadcasted_iota(jnp.int32, (C, H * C), 1)
    lb, ci = li // C, li % C
    ti = PHASES * (ri % rows_per_chunk) + ri // rows_per_chunk
    tj = PHASES * (ci % rows_per_chunk) + ci // rows_per_chunk
    causal = ti >= tj
    strict = ti > tj
    eye = jnp.where(ri == ci, 1.0, 0.0).astype(F32)
    diag_mask = (ti // 16) == (tj // 16)
    off_masks = [((ti // (2 * s)) == (tj // (2 * s))) & ((ti // s) > (tj // s)) for s in (16, 32)]
    lane_masks = [jnp.where(lb == u, 1.0, 0.0).astype(BF16) for u in range(H)]
    lower1 = jnp.where(causal[:, :C], 1.0, 0.0).astype(BF16)
    upper4 = jnp.where(ti <= tj, 1.0, 0.0).astype(BF16)
    eye1 = eye[:, :C].astype(BF16)
    lb_row = lb[0:1, :]
    ng = ng_ref[...]
    zero_k = jnp.zeros((C, Dh), BF16)

    def load(ref, b, c, h):
        r0 = c * rows_per_chunk
        return jnp.concatenate([ref[b, r0:r0 + rows_per_chunk, p * W + h * Dh:p * W + (h + 1) * Dh]
                                for p in range(PHASES)], axis=0)

    def pick_head(parts):
        out = parts[H - 1]
        for u in range(H - 2, -1, -1):
            out = jnp.where(lb == u, parts[u], out)
        return out

    groups = [(c, b) for c in range(n_chunks) for b in range(nb)]
    A4s, KKs, decays, rhss, qgss, kdecss, elasts = [], [], [], [], [], [], []
    for c, b in groups:
        gbc = gb_ref[b, c]
        gc_rows = _dot01(gbc, upper4)
        gc_col = _dot01(gbc, lower1, NT_DIMS)
        gb_col = _dot01(gbc, eye1, NT_DIMS)
        grow4 = gc_rows[H - 1:H, :]
        for u in range(H - 2, -1, -1):
            grow4 = jnp.where(lb_row == u, gc_rows[u:u + 1, :], grow4)
        gcol4 = pick_head([gc_col[:, u:u + 1] for u in range(H)])
        decays.append(jnp.where(causal, jnp.exp(jnp.where(causal, gcol4 - grow4, 0.0)), 0.0))
        kbqs, kks, rhs, qgs, kdecs, els = [], [], [], [], [], []
        for h in range(H):
            qh = load(q_ref, b, c, h)
            kh = load(k_ref, b, c, h)
            vh = load(v_ref, b, c, h)
            gcol = gc_col[:, h:h + 1]
            bcol = gb_col[:, H + h:H + h + 1]
            g_last = gc_rows[h:h + 1, C - 1:C]
            egc = jnp.exp(gcol)
            kb = kh * bcol
            kbqs.append(jnp.concatenate([kb, qh], axis=0).astype(BF16))
            kb16 = kh.astype(BF16)
            kks.append(jnp.concatenate([kb16 if u == h else zero_k for u in range(H)], axis=1))
            rhs.append(jnp.concatenate([vh * bcol, kb * egc], axis=1).astype(BF16))
            qgs.append(qh * egc)
            kdecs.append(kh * jnp.exp(g_last - gcol))
            els.append(jnp.exp(g_last))
        A4s.append(jnp.concatenate(kbqs, axis=1))
        KKs.append(jnp.concatenate(kks, axis=0))
        rhss.append(jnp.concatenate(rhs, axis=0))
        qgss.append(qgs)
        kdecss.append(kdecs)
        elasts.append(els)
    prods = [lax.dot_general(a, kk, NT_DIMS, preferred_element_type=F32) for a, kk in zip(A4s, KKs)]
    Ls = [jnp.where(strict, p[:C] * d, 0.0) for p, d in zip(prods, decays)]
    attns = [_block_diag(p[C:] * d, lane_masks) for p, d in zip(prods, decays)]
    Ts = _packed_lower_inverse(Ls, eye, diag_mask, off_masks, lane_masks)
    sols = [jnp.dot(_block_diag(T, lane_masks), r, preferred_element_type=F32) for T, r in zip(Ts, rhss)]

    for c in range(n_chunks):
        gi = [c * nb + b for b in range(nb)]
        Ss = [[s_ref[b * H + h] for h in range(H)] for b in range(nb)]
        ws_qs = [[_mm(jnp.concatenate([sols[g][h * C:(h + 1) * C, Dh:], qgss[g][h]], axis=0), Ss[b][h])
                  for h in range(H)] for b, g in enumerate(gi)]
        v_new = [[(sols[g][h * C:(h + 1) * C, :Dh] - ws_qs[b][h][:C]).astype(BF16) for h in range(H)]
                 for b, g in enumerate(gi)]
        o_att = [jnp.dot(attns[g], jnp.concatenate(v_new[b], axis=0), preferred_element_type=F32)
                 for b, g in enumerate(gi)]
        for b, g in enumerate(gi):
            for h in range(H):
                s_ref[b * H + h] = Ss[b][h] * elasts[g][h] + _mm_tn(kdecss[g][h], v_new[b][h])
        r0 = c * rows_per_chunk
        for b, g in enumerate(gi):
            for h in range(H):
                o = ws_qs[b][h][C:] + o_att[b][h * C:(h + 1) * C]
                on = _rms(o, ng) * load(gate_ref, b, c, h)
                for p in range(PHASES):
                    o_ref[b, r0:r0 + rows_per_chunk, p * W + h * Dh:p * W + (h + 1) * Dh] = (
                        on[p * rows_per_chunk:(p + 1) * rows_per_chunk])


def _gdn(q, k, v, gate, gb, norm_g, *, tg):
    B, TP, WP = q.shape
    n_chunks = tg // CHUNK
    row_spec = pl.BlockSpec((B, tg // PHASES, WP), lambda i: (0, i, 0))
    return pl.pallas_call(
        functools.partial(_gdn_kernel, n_chunks=n_chunks),
        grid=(TP * PHASES // tg,),
        in_specs=[row_spec, row_spec, row_spec, row_spec,
                  pl.BlockSpec((B, n_chunks, 2 * GDN_HEADS, CHUNK), lambda i: (0, i, 0, 0)),
                  _const_spec(norm_g.shape)],
        out_specs=row_spec,
        out_shape=jax.ShapeDtypeStruct((B, TP, WP), F32),
        scratch_shapes=[pltpu.VMEM((B * GDN_HEADS, GDN_HEAD_DIM, GDN_HEAD_DIM), F32)],
        compiler_params=_params(1),
        name="gdn",
    )(q, k, v, gate, gb, norm_g)


def _attn_kernel(x_ref, og_ref, c_ref, g_ref, wout_ref, kv_ref, wq_ref, wo_ref, h_ref, *, n_split):
    D = x_ref.shape[2] // PHASES
    R = x_ref.shape[1]
    dh = D // XATTN_HEADS
    per = PHASES // n_split
    groups = [range(i * per, (i + 1) * per) for i in range(n_split)]
    hs = range(XATTN_HEADS)

    def rows(ref, grp):
        w = ref.shape[2] // PHASES
        return jnp.concatenate([ref[0, :, v * w:(v + 1) * w] for v in grp], axis=0)

    mixes = [jnp.concatenate([rows(og_ref, grp).astype(BF16), rows(c_ref, grp).astype(BF16)], axis=1)
             for grp in groups]
    ys = [jnp.dot(m, wout_ref[...], preferred_element_type=F32) for m in mixes]
    h1s = [rows(x_ref, grp) + _rms(y, g_ref[1:2, :]) for grp, y in zip(groups, ys)]
    hns = [_rms(h1, g_ref[2:3, :]).astype(BF16) for h1 in h1s]
    qs = [(jnp.dot(hn, wq_ref[...], preferred_element_type=F32) * (dh ** -0.5)).astype(BF16) for hn in hns]
    ss = [[_mm_nt(q[:, hd * dh:(hd + 1) * dh], kv_ref[0, :, hd * dh:(hd + 1) * dh]) for hd in hs] for q in qs]
    es = [[jnp.exp(s - jnp.max(s, axis=-1, keepdims=True)) for s in sg] for sg in ss]
    pvs = [[_mm(e, kv_ref[0, :, D + hd * dh:D + (hd + 1) * dh]) for e, hd in zip(eg, hs)] for eg in es]
    os_ = [jnp.concatenate([o * (1.0 / jnp.sum(e, axis=-1, keepdims=True)) for o, e in zip(pg, eg)], axis=1)
           for pg, eg in zip(pvs, es)]
    y2s = [_mm(o, wo_ref[...]) for o in os_]
    for grp, h1, y2 in zip(groups, h1s, y2s):
        out = h1 + _rms(y2, g_ref[3:4, :])
        for i, v in enumerate(grp):
            h_ref[0, :, v * D:(v + 1) * D] = out[i * R:(i + 1) * R, :]


def _attn(x, o_gdn, c, norm_g, w_out, kv, w_q, w_o, *, tm):
    B, TP, DP = x.shape
    M = kv.shape[1]
    row_spec = lambda a: pl.BlockSpec((1, tm // PHASES, a.shape[2]), lambda b, i: (b, i, 0))
    return pl.pallas_call(
        functools.partial(_attn_kernel, n_split=tm // ROW_GROUP),
        grid=(B, TP * PHASES // tm),
        in_specs=[row_spec(x), row_spec(o_gdn), row_spec(c), _const_spec(norm_g.shape),
                  _const_spec(w_out.shape), pl.BlockSpec((1, M, kv.shape[2]), lambda b, i: (b, 0, 0)),
                  _const_spec(w_q.shape), _const_spec(w_o.shape)],
        out_specs=row_spec(x),
        out_shape=jax.ShapeDtypeStruct((B, TP, DP), F32),
        compiler_params=_params(2),
        name="attn",
    )(x, o_gdn, c, norm_g, w_out, kv, w_q, w_o)


def _ffn_kernel(h_ref, g_ref, wup_ref, cw_ref, cb_ref, wdn_ref, o_ref, carry, *, tm, fc, conv_k):
    F = wdn_ref.shape[0]
    D = wdn_ref.shape[1]
    n_chunks = F // fc
    R = tm // PHASES
    first_tail = PHASES - (conv_k - 1)

    @pl.when(pl.program_id(1) == 0)
    def _():
        carry[...] = jnp.zeros(carry.shape, F32)

    n_split = tm // ROW_GROUP
    Rg = R // n_split
    groups = range(n_split)
    rows = [slice(i * Rg, (i + 1) * Rg) for i in groups]
    slabs_in = _phase_slabs(h_ref, D)
    h2s = [jnp.concatenate([sl[rows[i], :] for sl in slabs_in], axis=0) for i in groups]
    hns = [_rms(h2, g_ref[4:5, :]).astype(BF16) for h2 in h2s]
    sub = lax.broadcasted_iota(jnp.int32, (1, SUBLANES, fc), 1)

    def up(j):
        return [[jnp.dot(hn, wup_ref[:, lo:lo + fc], preferred_element_type=F32) for lo in (j * fc, F + j * fc)]
                for hn in hns]

    def conv(us, lo):
        tails = {w: carry[w - first_tail, :, lo:lo + fc] for w in range(first_tail, PHASES)}
        taps = [cw_ref[k:k + 1, lo:lo + fc] for k in range(conv_k)]
        outs = []
        for u in us:
            slabs = [u[v * Rg:(v + 1) * Rg, :] for v in range(PHASES)]
            outs.append(jnp.concatenate(_phase_conv(slabs, tails, taps, sub, bias=cb_ref[:, lo:lo + fc]), axis=0))
            tails = {w: slabs[w][Rg - SUBLANES:, :] for w in range(first_tail, PHASES)}
        for w in range(first_tail, PHASES):
            carry[w - first_tail, :, lo:lo + fc] = tails[w]
        return outs

    acts = [[] for _ in groups]
    nxt = up(0)
    for j in range(n_chunks):
        cur = nxt
        if j + 1 < n_chunks:
            nxt = up(j + 1)
        ca = conv([cur[i][0] for i in groups], j * fc)
        cb = conv([cur[i][1] for i in groups], F + j * fc)
        for i in groups:
            acts[i].append((_silu(ca[i]) * cb[i]).astype(BF16))
    accs = [jnp.dot(jnp.concatenate(acts[i], axis=1), wdn_ref[...], preferred_element_type=F32) for i in groups]
    outs = [h2s[i] + _rms(accs[i], g_ref[5:6, :]) for i in groups]
    out3 = jnp.concatenate([o.reshape(PHASES, Rg, D) for o in outs], axis=1)
    o_ref[0] = pltpu.einshape("vjc->jvc", out3).reshape(tm, D)


def _ffn(h, norm_g, w_up, conv_w, conv_b, w_down, *, tm, fc):
    B, TP, DP = h.shape
    T, D = TP * PHASES, DP // PHASES
    F = w_down.shape[0]
    conv_k = conv_w.shape[0]
    assert conv_k - 1 < PHASES and tm % (PHASES * SUBLANES) == 0
    row_spec = pl.BlockSpec((1, tm // PHASES, PHASES * D), lambda b, i: (b, i, 0))
    kernel = functools.partial(_ffn_kernel, tm=tm, fc=fc, conv_k=conv_k)
    return pl.pallas_call(
        kernel,
        grid=(B, T // tm),
        in_specs=[row_spec, _const_spec(norm_g.shape), _const_spec(w_up.shape), _const_spec(conv_w.shape),
                  _const_spec(conv_b.shape), _const_spec(w_down.shape)],
        out_specs=pl.BlockSpec((1, tm, D), lambda b, i: (b, i, 0)),
        out_shape=jax.ShapeDtypeStruct((B, T, D), F32),
        scratch_shapes=[pltpu.VMEM((conv_k - 1, SUBLANES, 2 * F), F32)],
        compiler_params=_params(2),
        name="ffn",
    )(h, norm_g, w_up, conv_w, conv_b, w_down)


def _pick_tile(T, pref):
    t = min(T, pref)
    assert T % t == 0 and t % ROW_GROUP == 0 and ROW_GROUP % CHUNK == 0, (T, t)
    return t


def _layer(h, mem, g, w_in, gdn_conv_w, a_log, dt_bias, gdn_norm_g, cfm_dw_w, cfm_dw_b, cfm_ln_g, cfm_ln_b,
           w_out, mem_norm_g, xa_w_q, xa_w_kv, xa_w_o, ffn_w_up, ffn_conv_w, ffn_conv_b, ffn_w_down):
    B, T, D = h.shape
    s2 = 4 * GDN_WIDTH
    s4 = s2 + 2 * GDN_HEADS
    w_qkvg = w_in[:, :s2].astype(BF16)
    w_ab = w_in[:, s2:s4].T.astype(BF16)
    w_cfm = w_in[:, s4:].astype(BF16)
    pad = jnp.zeros((GDN_HEADS,), F32)
    a_log8 = jnp.concatenate([a_log.astype(F32), pad])[:, None]
    dt_bias8 = jnp.concatenate([dt_bias.astype(F32), pad])[:, None]

    kv = _mem_kv(mem, mem_norm_g[None, :], xa_w_kv.astype(BF16))
    q, k, v, gate, gb, c, hf = _mixer_in(h, g[0:1], w_qkvg, w_ab, w_cfm, gdn_conv_w, a_log8, dt_bias8,
                                         cfm_dw_w, cfm_dw_b[None, :], cfm_ln_g[None, :], cfm_ln_b[None, :],
                                         tm=_pick_tile(T, MIXER_TILE))
    o_gdn = _gdn(q, k, v, gate, gb, gdn_norm_g[None, :], tg=_pick_tile(T, GDN_TILE))
    hf = _attn(hf, o_gdn, c, g, w_out.astype(BF16), kv, xa_w_q.astype(BF16), xa_w_o.astype(BF16),
               tm=_pick_tile(T, TOKEN_TILE))
    return _ffn(hf, g, ffn_w_up.astype(BF16), ffn_conv_w, ffn_conv_b[None, :], ffn_w_down.astype(BF16),
                tm=_pick_tile(T, TOKEN_TILE), fc=FFN_CHUNK)


def kernel(x, mem, norm_g, w_in, gdn_conv_w, gdn_a_log, gdn_dt_bias, gdn_norm_g, cfm_dw_w, cfm_dw_b,
           cfm_ln_g, cfm_ln_b, w_out, mem_norm_g, xa_w_q, xa_w_kv, xa_w_o, ffn_w_up, ffn_conv_w,
           ffn_conv_b, ffn_w_down):
    h = x
    for l in range(norm_g.shape[0]):
        h = _layer(h, mem, norm_g[l], w_in[l], gdn_conv_w[l], gdn_a_log[l], gdn_dt_bias[l], gdn_norm_g[l],
                   cfm_dw_w[l], cfm_dw_b[l], cfm_ln_g[l], cfm_ln_b[l], w_out[l], mem_norm_g[l],
                   xa_w_q[l], xa_w_kv[l], xa_w_o[l], ffn_w_up[l], ffn_conv_w[l], ffn_conv_b[l], ffn_w_down[l])
    return h
```
